```python
import math
import jax, jax.numpy as jnp
from jax import lax
import numpy as np

D_MODEL = 1024
BATCH = 8
SEQ = 2048
DEPTH = 2

CHUNK = 64
D_MIX = D_MODEL
CONV_GROUPS = 8
CONV_GROUP_DIM = 64
W_CONV = CONV_GROUPS * CONV_GROUP_DIM
CONV_K = 3
LRU_HEADS = 8
LRU_HEAD_DIM = 64
W_LRU = LRU_HEADS * LRU_HEAD_DIM
LRU_CONV_K = 4
RG_C = 8.0
D_IN_TOT = 3 * W_CONV + 2 * W_LRU
D_FF = 3584
N_EXPERTS = 8
TOP_K = 2
N_DENSE = (DEPTH + 1) // 2
N_MOE = DEPTH // 2
EPS = 1e-6

kernel_name = "hybrid_shortconv_rglru_moe_trunk"


def rms_norm(x, g):
    xf = x.astype(jnp.float32)
    y = xf * lax.rsqrt(jnp.mean(xf * xf, axis=-1, keepdims=True) + EPS)
    return (y * g.astype(jnp.float32)).astype(x.dtype)


def causal_depthwise_conv(x, w):
    k = w.shape[0]
    c = x.shape[-1]
    return lax.conv_general_dilated(
        x, w[:, None, :].astype(x.dtype), window_strides=(1,), padding=[(k - 1, 0)],
        dimension_numbers=("NWC", "WIO", "NWC"), feature_group_count=c)


def chunked_linear_scan(a, b):
    bsz, s, w = a.shape
    nc = s // CHUNK
    a_c = a.reshape(bsz, nc, CHUNK, w).transpose(1, 0, 2, 3)
    b_c = b.reshape(bsz, nc, CHUNK, w).transpose(1, 0, 2, 3)

    def combine(left, right):
        al, bl = left
        ar, br = right
        return al * ar, ar * bl + br

    def step(h_prev, ab):
        ac, bc = ab
        a_cum, b_cum = lax.associative_scan(combine, (ac, bc), axis=1)
        h = a_cum * h_prev[:, None, :] + b_cum
        return h[:, -1, :], h

    h0 = jnp.zeros((bsz, w), jnp.float32)
    _, hs = lax.scan(step, h0, (a_c, b_c))
    return hs.transpose(1, 0, 2, 3).reshape(bsz, s, w)


def rg_lru(x, wa, ba, wx, bx, lam):
    bsz, s, _ = x.shape
    xh = x.reshape(bsz, s, LRU_HEADS, LRU_HEAD_DIM)
    r = jax.nn.sigmoid((jnp.einsum("bshi,hij->bshj", xh, wa).reshape(bsz, s, W_LRU) + ba).astype(jnp.float32))
    i = jax.nn.sigmoid((jnp.einsum("bshi,hij->bshj", xh, wx).reshape(bsz, s, W_LRU) + bx).astype(jnp.float32))
    log_a = -RG_C * r * jax.nn.softplus(-lam.astype(jnp.float32))
    a = jnp.exp(log_a)
    b = jnp.sqrt(-jnp.expm1(2.0 * log_a)) * (i * x.astype(jnp.float32))
    return chunked_linear_scan(a, b).astype(x.dtype)


def hybrid_mixer(h, w_in, conv_w, lru_conv_w, lru_conv_b, wa, ba, wx, bx, lam, w_out):
    p = h @ w_in
    c_gate, b_gate, v, u, g = jnp.split(
        p, [W_CONV, 2 * W_CONV, 3 * W_CONV, 3 * W_CONV + W_LRU], axis=-1)
    y_a = b_gate * causal_depthwise_conv(c_gate * v, conv_w)
    uc = causal_depthwise_conv(u, lru_conv_w) + lru_conv_b
    y_b = rg_lru(uc, wa, ba, wx, bx, lam) * jax.nn.gelu(g)
    return jnp.concatenate([y_a, y_b], axis=-1) @ w_out


def swiglu(x, wg, wu, wd):
    return (jax.nn.silu(x @ wg) * (x @ wu)) @ wd


def moe_swiglu(h, w_router, wg, wu, wd):
    bsz, s, d = h.shape
    xt = h.reshape(-1, d)
    logits = (xt @ w_router).astype(jnp.float32)
    top_v, top_i = lax.top_k(logits, TOP_K)
    gates = jax.nn.softmax(top_v, axis=-1)
    comb = jnp.sum(jax.nn.one_hot(top_i, N_EXPERTS, dtype=jnp.float32) * gates[..., None], axis=1)
    comb = comb.astype(h.dtype)
    y = jnp.zeros_like(xt)
    for e in range(N_EXPERTS):
        y = y + comb[:, e:e + 1] * swiglu(xt, wg[e], wu[e], wd[e])
    return y.reshape(bsz, s, d)


def setup_inputs(seed: int = 0) -> dict:
    key = jax.random.key(seed)
    ks = jax.random.split(key, 24)
    f32 = jnp.float32
    nrm = lambda k, shape, fan_in: jax.random.normal(k, shape, f32) * (fan_in ** -0.5)
    u = jax.random.uniform(ks[11], (DEPTH, W_LRU), f32, minval=0.9, maxval=0.999)
    a0 = u ** (1.0 / RG_C)
    lam = jnp.log(a0) - jnp.log1p(-a0)
    return {
        "x": jax.random.normal(ks[0], (BATCH, SEQ, D_MODEL), f32),
        "norm_mix": 1.0 + 0.02 * jax.random.normal(ks[1], (DEPTH, D_MODEL), f32),
        "norm_ffn": 1.0 + 0.02 * jax.random.normal(ks[2], (DEPTH, D_MODEL), f32),
        "norm_final": 1.0 + 0.02 * jax.random.normal(ks[3], (D_MODEL,), f32),
        "w_in": nrm(ks[4], (DEPTH, D_MODEL, D_IN_TOT), D_MODEL),
        "conv_w": nrm(ks[5], (DEPTH, CONV_K, W_CONV), CONV_K),
        "lru_conv_w": nrm(ks[6], (DEPTH, LRU_CONV_K, W_LRU), LRU_CONV_K),
        "lru_conv_b": 0.02 * jax.random.normal(ks[7], (DEPTH, W_LRU), f32),
        "lru_wa": nrm(ks[8], (DEPTH, LRU_HEADS, LRU_HEAD_DIM, LRU_HEAD_DIM), LRU_HEAD_DIM),
        "lru_ba": 0.02 * jax.random.normal(ks[9], (DEPTH, W_LRU), f32),
        "lru_wx": nrm(ks[10], (DEPTH, LRU_HEADS, LRU_HEAD_DIM, LRU_HEAD_DIM), LRU_HEAD_DIM),
        "lru_bx": 0.02 * jax.random.normal(ks[12], (DEPTH, W_LRU), f32),
        "lru_lambda": lam,
        "w_out": nrm(ks[13], (DEPTH, D_MIX, D_MODEL), D_MIX),
        "ffn_w_gate": nrm(ks[14], (N_DENSE, D_MODEL, D_FF), D_MODEL),
        "ffn_w_up": nrm(ks[15], (N_DENSE, D_MODEL, D_FF), D_MODEL),
        "ffn_w_down": nrm(ks[16], (N_DENSE, D_FF, D_MODEL), D_FF),
        "w_router": nrm(ks[17], (N_MOE, D_MODEL, N_EXPERTS), D_MODEL),
        "moe_w_gate": nrm(ks[18], (N_MOE, N_EXPERTS, D_MODEL, D_FF), D_MODEL),
        "moe_w_up": nrm(ks[19], (N_MOE, N_EXPERTS, D_MODEL, D_FF), D_MODEL),
        "moe_w_down": nrm(ks[20], (N_MOE, N_EXPERTS, D_FF, D_MODEL), D_FF),
    }


def reference(x, norm_mix, norm_ffn, norm_final, w_in, conv_w, lru_conv_w, lru_conv_b,
              lru_wa, lru_ba, lru_wx, lru_bx, lru_lambda, w_out,
              ffn_w_gate, ffn_w_up, ffn_w_down,
              w_router, moe_w_gate, moe_w_up, moe_w_down):
    for l in range(DEPTH):
        h = rms_norm(x, norm_mix[l])
        x = x + hybrid_mixer(h, w_in[l], conv_w[l], lru_conv_w[l], lru_conv_b[l],
                             lru_wa[l], lru_ba[l], lru_wx[l], lru_bx[l], lru_lambda[l], w_out[l])
        h = rms_norm(x, norm_ffn[l])
        j = l // 2
        if l % 2 == 0:
            x = x + swiglu(h, ffn_w_gate[j], ffn_w_up[j], ffn_w_down[j])
        else:
            x = x + moe_swiglu(h, w_router[j], moe_w_gate[j], moe_w_up[j], moe_w_down[j])
    return rms_norm(x, norm_final)
```

```python
import functools

import jax
import jax.numpy as jnp
from jax import lax
from jax.experimental import pallas as pl
from jax.experimental.pallas import tpu as pltpu

F32 = jnp.float32
BF16 = jnp.bfloat16
I32 = jnp.int32

EPS = 1e-6
RG_C = 8.0
W_CONV = 512
W_LRU = 512
N_EXPERTS = 8
TOP_K = 2

SUBLANES = 8
LANES = 128
MXU_DIM = 256
VMEM_LIMIT_BYTES = 56 * 1024 * 1024

TM_MIX = 256
TM_FFN = 512
FF_CHUNK = 512
TM_ROUTE = 512
TM_DISP = 256
TM_EXP = 256


def _const_spec(shape):
    nd = len(shape)
    return pl.BlockSpec(shape, lambda *_: (0,) * nd, pipeline_mode=pl.Buffered(1))


def _rms_norm(x, g):
    ms = jnp.mean(x * x, axis=-1, keepdims=True)
    return x * lax.rsqrt(ms + EPS) * g


def _sigmoid(x):
    return 1.0 / (1.0 + jnp.exp(-x))


def _gelu_tanh(x):
    c = 0.7978845608028654
    return 0.5 * x * (1.0 + jnp.tanh(c * (x + 0.044715 * (x * x * x))))


def _mixer_kernel(x_ref, gn_ref, win_ref, cw_ref, lcw_ref, lcb_ref, wgate_ref, ba_ref, bx_ref,
                  lam_ref, wout_ref, o_ref, zbuf, ubuf, abuf, bbuf, hbuf, hstate):
    j = pl.program_id(1)
    tm = x_ref.shape[0]
    halo = SUBLANES

    @pl.when(j == 0)
    def _():
        zbuf[0:halo, :] = jnp.zeros((halo, W_CONV), F32)
        ubuf[0:halo, :] = jnp.zeros((halo, W_LRU), F32)
        hstate[...] = jnp.zeros((SUBLANES, W_LRU), F32)

    x = x_ref[...]
    h = _rms_norm(x, gn_ref[...]).astype(BF16)
    p = jnp.dot(h, win_ref[...], preferred_element_type=F32)
    c_gate = p[:, 0:W_CONV]
    b_gate = p[:, W_CONV:2 * W_CONV]
    v = p[:, 2 * W_CONV:3 * W_CONV]
    u = p[:, 3 * W_CONV:3 * W_CONV + W_LRU]
    g = p[:, 3 * W_CONV + W_LRU:]

    z = c_gate * v
    zbuf[halo:halo + tm, :] = z
    cw = cw_ref[...]
    conv = cw[2:3, :] * z + cw[1:2, :] * zbuf[halo - 1:halo - 1 + tm, :] + cw[0:1, :] * zbuf[halo - 2:halo - 2 + tm, :]
    y_a = b_gate * conv
    zbuf[0:halo, :] = zbuf[tm:tm + halo, :]

    ubuf[halo:halo + tm, :] = u
    lw = lcw_ref[...]
    uc = (lw[3:4, :] * u + lw[2:3, :] * ubuf[halo - 1:halo - 1 + tm, :]
          + lw[1:2, :] * ubuf[halo - 2:halo - 2 + tm, :] + lw[0:1, :] * ubuf[halo - 3:halo - 3 + tm, :]
          + lcb_ref[...])
    ubuf[0:halo, :] = ubuf[tm:tm + halo, :]

    ucb = uc.astype(BF16)
    pre0 = jnp.dot(ucb[:, 0:MXU_DIM], wgate_ref[0], preferred_element_type=F32)
    pre1 = jnp.dot(ucb[:, MXU_DIM:], wgate_ref[1], preferred_element_type=F32)
    pre_a = jnp.concatenate([pre0[:, :MXU_DIM], pre1[:, :MXU_DIM]], axis=1) + ba_ref[...]
    pre_x = jnp.concatenate([pre0[:, MXU_DIM:], pre1[:, MXU_DIM:]], axis=1) + bx_ref[...]
    r = _sigmoid(pre_a)
    i_gate = _sigmoid(pre_x)
    neg_lam = -lam_ref[...]
    softplus = jnp.maximum(neg_lam, 0.0) + jnp.log1p(jnp.exp(-jnp.abs(neg_lam)))
    log_a = (-RG_C) * r * softplus
    a = jnp.exp(log_a)
    th = jnp.tanh(log_a)
    b = jnp.sqrt((-2.0 * th) / (1.0 - th)) * (i_gate * uc)

    ng = tm // SUBLANES
    a3 = a.reshape(ng, SUBLANES, W_LRU)
    b3 = b.reshape(ng, SUBLANES, W_LRU)
    row = lax.broadcasted_iota(I32, (ng, SUBLANES, W_LRU), 1)
    for s in (1, 2, 4):
        keep = row >= s
        b3 = b3 + jnp.where(keep, a3, 0.0) * pltpu.roll(b3, s, axis=1)
        a3 = a3 * jnp.where(keep, pltpu.roll(a3, s, axis=1), 1.0)
    abuf[...] = a3.reshape(tm, W_LRU)
    bbuf[...] = b3.reshape(tm, W_LRU)

    def group_step(k, h_prev):
        r0 = pl.multiple_of(k * SUBLANES, SUBLANES)
        hk = abuf[pl.ds(r0, SUBLANES), :] * h_prev + bbuf[pl.ds(r0, SUBLANES), :]
        hbuf[pl.ds(r0, SUBLANES), :] = hk
        return jnp.broadcast_to(hk[SUBLANES - 1:SUBLANES, :], (SUBLANES, W_LRU))

    hstate[...] = lax.fori_loop(0, ng, group_step, hstate[...], unroll=8)

    y_b = hbuf[...] * _gelu_tanh(g)
    y = jnp.concatenate([y_a, y_b], axis=1).astype(BF16)
    o_ref[...] = x + jnp.dot(y, wout_ref[...], preferred_element_type=F32)


def _block_diag_halves(w):
    nh, dh, _ = w.shape
    per = MXU_DIM // dh
    w4 = w.reshape(nh // per, per, dh, dh)
    eye = jnp.eye(per, dtype=w.dtype)
    return jnp.einsum("kaij,ab->kaibj", w4, eye).reshape(nh // per, MXU_DIM, MXU_DIM)


def _mixer(x, gn, w_in, conv_w, lru_conv_w, lru_conv_b, wa, ba, wx, bx, lam, w_out):
    bsz, seq, d = x.shape
    d_in = w_in.shape[1]
    tm = TM_MIX
    wgate = jnp.concatenate([_block_diag_halves(wa), _block_diag_halves(wx)], axis=-1).astype(BF16)
    row = lambda a: a.reshape(1, -1)
    return pl.pallas_call(
        _mixer_kernel,
        grid=(bsz, seq // tm),
        in_specs=[
            pl.BlockSpec((None, tm, d), lambda b, j: (b, j, 0)),
            _const_spec((1, d)),
            _const_spec((d, d_in)),
            _const_spec(conv_w.shape),
            _const_spec(lru_conv_w.shape),
            _const_spec((1, W_LRU)),
            _const_spec(wgate.shape),
            _const_spec((1, W_LRU)),
            _const_spec((1, W_LRU)),
            _const_spec((1, W_LRU)),
            _const_spec(w_out.shape),
        ],
        out_specs=pl.BlockSpec((None, tm, d), lambda b, j: (b, j, 0)),
        out_shape=jax.ShapeDtypeStruct(x.shape, F32),
        scratch_shapes=[
            pltpu.VMEM((tm + 2 * SUBLANES, W_CONV), F32),
            pltpu.VMEM((tm + 2 * SUBLANES, W_LRU), F32),
            pltpu.VMEM((tm, W_LRU), F32),
            pltpu.VMEM((tm, W_LRU), F32),
            pltpu.VMEM((tm, W_LRU), F32),
            pltpu.VMEM((SUBLANES, W_LRU), F32),
        ],
        compiler_params=pltpu.CompilerParams(
            dimension_semantics=("arbitrary", "arbitrary"), vmem_limit_bytes=VMEM_LIMIT_BYTES),
        name="mixer",
    )(x, row(gn), w_in.astype(BF16), conv_w, lru_conv_w, row(lru_conv_b), wgate, row(ba), row(bx),
      row(lam), w_out.astype(BF16))


def _swiglu_rows(h, wg_ref, wu_ref, wd_ref):
    d_ff = wg_ref.shape[-1]
    acc = jnp.zeros((h.shape[0], wd_ref.shape[-1]), F32)
    for c in range(d_ff // FF_CHUNK):
        cols = slice(c * FF_CHUNK, (c + 1) * FF_CHUNK)
        gate = jnp.dot(h, wg_ref[:, cols], preferred_element_type=F32)
        up = jnp.dot(h, wu_ref[:, cols], preferred_element_type=F32)
        act = (gate * _sigmoid(gate) * up).astype(BF16)
        acc = acc + jnp.dot(act, wd_ref[cols, :], preferred_element_type=F32)
    return acc


def _ffn_kernel(x_ref, gn_ref, wg_ref, wu_ref, wd_ref, o_ref):
    x = x_ref[...]
    h = _rms_norm(x, gn_ref[...]).astype(BF16)
    o_ref[...] = x + _swiglu_rows(h, wg_ref, wu_ref, wd_ref)


def _ffn_dense(x2, gn, wg, wu, wd):
    t, d = x2.shape
    tm = TM_FFN
    return pl.pallas_call(
        _ffn_kernel,
        grid=(t // tm,),
        in_specs=[
            pl.BlockSpec((tm, d), lambda i: (i, 0)),
            _const_spec((1, d)),
            _const_spec(wg.shape),
            _const_spec(wu.shape),
            _const_spec(wd.shape),
        ],
        out_specs=pl.BlockSpec((tm, d), lambda i: (i, 0)),
        out_shape=jax.ShapeDtypeStruct(x2.shape, F32),
        compiler_params=pltpu.CompilerParams(
            dimension_semantics=("arbitrary",), vmem_limit_bytes=VMEM_LIMIT_BYTES),
        name="ffn_dense",
    )(x2, gn.reshape(1, -1), wg.astype(BF16), wu.astype(BF16), wd.astype(BF16))


def _router_kernel(x_ref, gn_ref, wr_ref, meta_ref, gates_ref, counts_ref, run):
    i = pl.program_id(0)
    tm = x_ref.shape[0]

    @pl.when(i == 0)
    def _():
        run[...] = jnp.zeros((1, LANES), F32)

    h = _rms_norm(x_ref[...], gn_ref[...])
    logits = jnp.dot(h, wr_ref[...], preferred_element_type=F32, precision=lax.Precision.HIGHEST)
    lane = lax.broadcasted_iota(I32, (tm, LANES), 1)
    neg_inf = jnp.float32(-jnp.inf)
    logits = jnp.where(lane < N_EXPERTS, logits, neg_inf)
    m1 = jnp.max(logits, axis=-1, keepdims=True)
    i1 = jnp.min(jnp.where(logits == m1, lane, LANES), axis=-1, keepdims=True)
    rest = jnp.where(lane == i1, neg_inf, logits)
    m2 = jnp.max(rest, axis=-1, keepdims=True)
    i2 = jnp.min(jnp.where(rest == m2, lane, LANES), axis=-1, keepdims=True)
    e = jnp.exp(m2 - m1)
    g1 = 1.0 / (1.0 + e)
    g2 = e * g1

    sel1 = lane == i1
    sel2 = lane == i2
    hot = jnp.where(sel1 | sel2, 1.0, 0.0)
    rows = lax.broadcasted_iota(I32, (tm, tm), 0)
    cols = lax.broadcasted_iota(I32, (tm, tm), 1)
    strict_lower = jnp.where(cols < rows, 1.0, 0.0).astype(BF16)
    before = jnp.dot(strict_lower, hot.astype(BF16), preferred_element_type=F32) + run[...]
    rank1 = jnp.sum(jnp.where(sel1, before, 0.0), axis=-1, keepdims=True)
    rank2 = jnp.sum(jnp.where(sel2, before, 0.0), axis=-1, keepdims=True)
    run[...] = run[...] + jnp.sum(hot, axis=0, keepdims=True)

    meta = jnp.where(lane == 0, i1, jnp.where(lane == 1, i2, jnp.where(
        lane == 2, rank1.astype(I32), jnp.where(lane == 3, rank2.astype(I32), 0))))
    meta_ref[...] = meta
    gates_ref[...] = jnp.where(lane == 0, g1, jnp.where(lane == 1, g2, 0.0))
    counts_ref[...] = jnp.broadcast_to(run[...], (SUBLANES, LANES))


def _router(x2, gn, w_router):
    t, d = x2.shape
    tm = TM_ROUTE
    wr = jnp.zeros((d, LANES), F32).at[:, :N_EXPERTS].set(w_router)
    return pl.pallas_call(
        _router_kernel,
        grid=(t // tm,),
        in_specs=[
            pl.BlockSpec((tm, d), lambda i: (i, 0)),
            _const_spec((1, d)),
            _const_spec((d, LANES)),
        ],
        out_specs=[
            pl.BlockSpec((tm, LANES), lambda i: (i, 0)),
            pl.BlockSpec((tm, LANES), lambda i: (i, 0)),
            pl.BlockSpec((SUBLANES, LANES), lambda i: (0, 0)),
        ],
        out_shape=[
            jax.ShapeDtypeStruct((t, LANES), I32),
            jax.ShapeDtypeStruct((t, LANES), F32),
            jax.ShapeDtypeStruct((SUBLANES, LANES), F32),
        ],
        scratch_shapes=[pltpu.VMEM((1, LANES), F32)],
        compiler_params=pltpu.CompilerParams(
            dimension_semantics=("arbitrary",), vmem_limit_bytes=VMEM_LIMIT_BYTES),
        name="router",
    )(x2, gn.reshape(1, -1), wr)


def _row_copy(src_ref, src_row, dst_ref, dst_row, sem):
    return pltpu.make_async_copy(src_ref.at[pl.ds(src_row, 1), :], dst_ref.at[pl.ds(dst_row, 1), :], sem)


def _dispatch_kernel(pad_ref, pos_ref, x_ref, xs_hbm, ztile, sem, zsem):
    i = pl.program_id(0)
    tm = x_ref.shape[0]
    tile_rows = ztile.shape[0]

    @pl.when(i == 0)
    def _():
        ztile[...] = jnp.zeros(ztile.shape, F32)
        for e in range(N_EXPERTS):
            start = pad_ref[0, e]
            n = pad_ref[1, e]

            def fill(q, c):
                _row_copy(ztile, 0, xs_hbm, start + q, zsem).start()
                return c

            def drain(q, c):
                _row_copy(ztile, 0, xs_hbm, start, zsem).wait()
                return c

            lax.fori_loop(0, n, fill, 0)
            lax.fori_loop(0, n, drain, 0)

        def tail_copy(q):
            r0 = pl.multiple_of(q * tile_rows, tile_rows)
            return pltpu.make_async_copy(ztile, xs_hbm.at[pl.ds(r0, tile_rows), :], zsem)

        def fill_tail(q, c):
            tail_copy(q).start()
            return c

        def drain_tail(q, c):
            tail_copy(q).wait()
            return c

        n_valid = pad_ref[2, 0]
        n_tiles = xs_hbm.shape[0] // tile_rows
        lax.fori_loop(n_valid, n_tiles, fill_tail, 0)
        lax.fori_loop(n_valid, n_tiles, drain_tail, 0)

    def issue(r, c):
        for k in range(TOP_K):
            _row_copy(x_ref, r, xs_hbm, pos_ref[0, 0, TOP_K * r + k], sem).start()
        return c

    def drain_all(r, c):
        for k in range(TOP_K):
            _row_copy(x_ref, 0, xs_hbm, 0, sem).wait()
        return c

    lax.fori_loop(0, tm, issue, 0)
    lax.fori_loop(0, tm, drain_all, 0)


def _dispatch(x2, pos, pad_info, n_rows):
    t, d = x2.shape
    tm = TM_DISP
    pos3 = pos.reshape(t // tm, 1, TOP_K * tm)
    grid_spec = pltpu.PrefetchScalarGridSpec(
        num_scalar_prefetch=1,
        grid=(t // tm,),
        in_specs=[
            pl.BlockSpec((1, 1, TOP_K * tm), lambda i, pad: (i, 0, 0), memory_space=pltpu.SMEM),
            pl.BlockSpec((tm, d), lambda i, pad: (i, 0)),
        ],
        out_specs=pl.BlockSpec(memory_space=pl.ANY),
        scratch_shapes=[
            pltpu.VMEM((TM_EXP, d), F32),
            pltpu.SemaphoreType.DMA(()),
            pltpu.SemaphoreType.DMA(()),
        ],
    )
    return pl.pallas_call(
        _dispatch_kernel,
        grid_spec=grid_spec,
        out_shape=jax.ShapeDtypeStruct((n_rows, d), F32),
        compiler_params=pltpu.CompilerParams(
            dimension_semantics=("arbitrary",), vmem_limit_bytes=VMEM_LIMIT_BYTES),
        name="dispatch",
    )(pad_info, pos3, x2)


def _expert_kernel(texp_ref, nvalid_ref, xs_ref, gn_ref, wg_ref, wu_ref, wd_ref, ys_ref):
    i = pl.program_id(0)

    @pl.when(i < nvalid_ref[0])
    def _():
        h = _rms_norm(xs_ref[...], gn_ref[...]).astype(BF16)
        ys_ref[...] = _swiglu_rows(h, wg_ref, wu_ref, wd_ref)

    @pl.when(i >= nvalid_ref[0])
    def _():
        ys_ref[...] = jnp.zeros(ys_ref.shape, F32)


def _experts(xs, gn, wg, wu, wd, tile_expert, n_valid):
    n_rows, d = xs.shape
    tm = TM_EXP
    d_ff = wg.shape[-1]
    tile = lambda i, te, nv: (jnp.minimum(i, nv[0] - 1), 0)
    expert = lambda i, te, nv: (te[jnp.minimum(i, nv[0] - 1)], 0, 0)
    grid_spec = pltpu.PrefetchScalarGridSpec(
        num_scalar_prefetch=2,
        grid=(n_rows // tm,),
        in_specs=[
            pl.BlockSpec((tm, d), tile),
            pl.BlockSpec((1, d), lambda i, te, nv: (0, 0)),
            pl.BlockSpec((None, d, d_ff), expert),
            pl.BlockSpec((None, d, d_ff), expert),
            pl.BlockSpec((None, d_ff, d), expert),
        ],
        out_specs=pl.BlockSpec((tm, d), lambda i, te, nv: (i, 0)),
    )
    return pl.pallas_call(
        _expert_kernel,
        grid_spec=grid_spec,
        out_shape=jax.ShapeDtypeStruct((n_rows, d), F32),
        compiler_params=pltpu.CompilerParams(
            dimension_semantics=("arbitrary",), vmem_limit_bytes=VMEM_LIMIT_BYTES),
        name="experts",
    )(tile_expert, n_valid, xs, gn.reshape(1, -1), wg.astype(BF16), wu.astype(BF16), wd.astype(BF16))


def _combine_kernel(pos_ref, x_ref, gates_ref, gf_ref, ys_hbm, o_ref, buf, sem):
    tm = x_ref.shape[0]

    def issue(r, c):
        for k in range(TOP_K):
            _row_copy(ys_hbm, pos_ref[0, 0, TOP_K * r + k], buf.at[k], r, sem).start()
        return c

    def drain(r, c):
        for k in range(TOP_K):
            _row_copy(ys_hbm, 0, buf.at[k], 0, sem).wait()
        return c

    lax.fori_loop(0, tm, issue, 0)
    lax.fori_loop(0, tm, drain, 0)
    gates = gates_ref[...]
    y = x_ref[...] + gates[:, 0:1] * buf[0] + gates[:, 1:2] * buf[1]
    o_ref[...] = _rms_norm(y, gf_ref[...])


def _combine(x2, gates, pos, ys, g_final):
    t, d = x2.shape
    tm = TM_DISP
    pos3 = pos.reshape(t // tm, 1, TOP_K * tm)
    return pl.pallas_call(
        _combine_kernel,
        grid=(t // tm,),
        in_specs=[
            pl.BlockSpec((1, 1, TOP_K * tm), lambda i: (i, 0, 0), memory_space=pltpu.SMEM),
            pl.BlockSpec((tm, d), lambda i: (i, 0)),
            pl.BlockSpec((tm, LANES), lambda i: (i, 0)),
            _const_spec((1, d)),
            pl.BlockSpec(memory_space=pl.ANY),
        ],
        out_specs=pl.BlockSpec((tm, d), lambda i: (i, 0)),
        out_shape=jax.ShapeDtypeStruct(x2.shape, F32),
        scratch_shapes=[pltpu.VMEM((TOP_K, tm, d), F32), pltpu.SemaphoreType.DMA(())],
        compiler_params=pltpu.CompilerParams(
            dimension_semantics=("arbitrary",), vmem_limit_bytes=VMEM_LIMIT_BYTES),
        name="combine",
    )(pos3, x2, gates, g_final.reshape(1, -1), ys)


def _moe_layer(x2, gn, w_router, wg, wu, wd, g_final):
    t, _ = x2.shape
    tm = TM_EXP
    n_tiles = (TOP_K * t) // tm + N_EXPERTS
    meta, gates, counts = _router(x2, gn, w_router)

    counts = counts[0, :N_EXPERTS].astype(I32)
    padded = ((counts + tm - 1) // tm) * tm
    ends = jnp.cumsum(padded)
    starts = ends - padded
    idx = meta[:, 0:TOP_K]
    rank = meta[:, TOP_K:2 * TOP_K]
    pos = (starts[idx] + rank).reshape(-1)
    tile_start = jnp.arange(n_tiles, dtype=I32) * tm
    tile_expert = jnp.minimum(
        jnp.sum((tile_start[:, None] >= ends[None, :]).astype(I32), axis=1), N_EXPERTS - 1).astype(I32)
    n_valid = (ends[-1:] // tm).astype(I32)
    pad_info = jnp.stack([starts + counts, padded - counts, jnp.broadcast_to(n_valid, (N_EXPERTS,))]).astype(I32)

    xs = _dispatch(x2, pos, pad_info, n_tiles * tm)
    ys = _experts(xs, gn, wg, wu, wd, tile_expert, n_valid)
    return _combine(x2, gates, pos, ys, g_final)


def kernel(x, norm_mix, norm_ffn, norm_final, w_in, conv_w, lru_conv_w, lru_conv_b, lru_wa, lru_ba, lru_wx,
           lru_bx, lru_lambda, w_out, ffn_w_gate, ffn_w_up, ffn_w_down, w_router, moe_w_gate, moe_w_up,
           moe_w_down):
    bsz, seq, d = x.shape
    depth = w_in.shape[0]
    assert depth == 2, "layer 0 uses the dense FFN, layer 1 the routed FFN followed by the final norm"
    for l in range(depth):
        x = _mixer(x, norm_mix[l], w_in[l], conv_w[l], lru_conv_w[l], lru_conv_b[l], lru_wa[l], lru_ba[l],
                   lru_wx[l], lru_bx[l], lru_lambda[l], w_out[l])
        x2 = x.reshape(bsz * seq, d)
        if l % 2 == 0:
            x = _ffn_dense(x2, norm_ffn[l], ffn_w_gate[l // 2], ffn_w_up[l // 2],
                           ffn_w_down[l // 2]).reshape(bsz, seq, d)
        else:
            x = _moe_layer(x2, norm_ffn[l], w_router[l // 2], moe_w_gate[l // 2], moe_w_up[l // 2],
                           moe_w_down[l // 2], norm_final).reshape(bsz, seq, d)
    return x
```

```python
import functools

import jax
import jax.numpy as jnp
from jax import lax
from jax.experimental import pallas as pl
from jax.experimental.pallas import tpu as pltpu

F32 = jnp.float32
BF16 = jnp.bfloat16
I32 = jnp.int32

EPS = 1e-6
RG_C = 8.0
W_CONV = 512
W_LRU = 512
N_EXPERTS = 8
TOP_K = 2

SUBLANES = 8
LANES = 128
MXU_DIM = 256
VMEM_LIMIT_BYTES = 56 * 1024 * 1024

TM_MIX = 256
TM_FFN = 512
FF_CHUNK = 512
TM_ROUTE = 512
TM_DISP = 512
ROW_DMA_UNROLL = 8
TM_EXP = 256


def _const_spec(shape):
    nd = len(shape)
    return pl.BlockSpec(shape, lambda *_: (0,) * nd, pipeline_mode=pl.Buffered(1))


def _rms_norm(x, g):
    ms = jnp.mean(x * x, axis=-1, keepdims=True)
    return x * lax.rsqrt(ms + EPS) * g


def _sigmoid(x):
    return 1.0 / (1.0 + jnp.exp(-x))


def _gelu_tanh(x):
    c = 0.7978845608028654
    return 0.5 * x * (1.0 + jnp.tanh(c * (x + 0.044715 * (x * x * x))))


def _mixer_kernel(x_ref, gn_ref, win_ref, cw_ref, lcw_ref, lcb_ref, wgate_ref, ba_ref, bx_ref,
                  lam_ref, wout_ref, o_ref, zbuf, ubuf, abuf, bbuf, hbuf, hstate):
    j = pl.program_id(1)
    tm = x_ref.shape[0]
    halo = SUBLANES

    @pl.when(j == 0)
    def _():
        zbuf[0:halo, :] = jnp.zeros((halo, W_CONV), F32)
        ubuf[0:halo, :] = jnp.zeros((halo, W_LRU), F32)
        hstate[...] = jnp.zeros((SUBLANES, W_LRU), F32)

    x = x_ref[...]
    h = _rms_norm(x, gn_ref[...]).astype(BF16)
    p = jnp.dot(h, win_ref[...], preferred_element_type=F32)
    c_gate = p[:, 0:W_CONV]
    b_gate = p[:, W_CONV:2 * W_CONV]
    v = p[:, 2 * W_CONV:3 * W_CONV]
    u = p[:, 3 * W_CONV:3 * W_CONV + W_LRU]
    g = p[:, 3 * W_CONV + W_LRU:]

    z = c_gate * v
    zbuf[halo:halo + tm, :] = z
    cw = cw_ref[...]
    conv = cw[2:3, :] * z + cw[1:2, :] * zbuf[halo - 1:halo - 1 + tm, :] + cw[0:1, :] * zbuf[halo - 2:halo - 2 + tm, :]
    y_a = b_gate * conv
    zbuf[0:halo, :] = zbuf[tm:tm + halo, :]

    ubuf[halo:halo + tm, :] = u
    lw = lcw_ref[...]
    uc = (lw[3:4, :] * u + lw[2:3, :] * ubuf[halo - 1:halo - 1 + tm, :]
          + lw[1:2, :] * ubuf[halo - 2:halo - 2 + tm, :] + lw[0:1, :] * ubuf[halo - 3:halo - 3 + tm, :]
          + lcb_ref[...])
    ubuf[0:halo, :] = ubuf[tm:tm + halo, :]

    ucb = uc.astype(BF16)
    pre0 = jnp.dot(ucb[:, 0:MXU_DIM], wgate_ref[0], preferred_element_type=F32)
    pre1 = jnp.dot(ucb[:, MXU_DIM:], wgate_ref[1], preferred_element_type=F32)
    pre_a = jnp.concatenate([pre0[:, :MXU_DIM], pre1[:, :MXU_DIM]], axis=1) + ba_ref[...]
    pre_x = jnp.concatenate([pre0[:, MXU_DIM:], pre1[:, MXU_DIM:]], axis=1) + bx_ref[...]
    r = _sigmoid(pre_a)
    i_gate = _sigmoid(pre_x)
    neg_lam = -lam_ref[...]
    softplus = jnp.maximum(neg_lam, 0.0) + jnp.log1p(jnp.exp(-jnp.abs(neg_lam)))
    log_a = (-RG_C) * r * softplus
    a = jnp.exp(log_a)
    th = jnp.tanh(log_a)
    b = jnp.sqrt((-2.0 * th) / (1.0 - th)) * (i_gate * uc)

    ng = tm // SUBLANES
    a3 = a.reshape(ng, SUBLANES, W_LRU)
    b3 = b.reshape(ng, SUBLANES, W_LRU)
    row = lax.broadcasted_iota(I32, (ng, SUBLANES, W_LRU), 1)
    for s in (1, 2, 4):
        keep = row >= s
        b3 = b3 + jnp.where(keep, a3, 0.0) * pltpu.roll(b3, s, axis=1)
        a3 = a3 * jnp.where(keep, pltpu.roll(a3, s, axis=1), 1.0)
    abuf[...] = a3.reshape(tm, W_LRU)
    bbuf[...] = b3.reshape(tm, W_LRU)

    def group_step(k, h_prev):
        r0 = pl.multiple_of(k * SUBLANES, SUBLANES)
        hk = abuf[pl.ds(r0, SUBLANES), :] * h_prev + bbuf[pl.ds(r0, SUBLANES), :]
        hbuf[pl.ds(r0, SUBLANES), :] = hk
        return jnp.broadcast_to(hk[SUBLANES - 1:SUBLANES, :], (SUBLANES, W_LRU))

    hstate[...] = lax.fori_loop(0, ng, group_step, hstate[...], unroll=8)

    y_b = hbuf[...] * _gelu_tanh(g)
    y = jnp.concatenate([y_a, y_b], axis=1).astype(BF16)
    o_ref[...] = x + jnp.dot(y, wout_ref[...], preferred_element_type=F32)


def _block_diag_halves(w):
    nh, dh, _ = w.shape
    per = MXU_DIM // dh
    w4 = w.reshape(nh // per, per, dh, dh)
    eye = jnp.eye(per, dtype=w.dtype)
    return jnp.einsum("kaij,ab->kaibj", w4, eye).reshape(nh // per, MXU_DIM, MXU_DIM)


def _mixer(x, gn, w_in, conv_w, lru_conv_w, lru_conv_b, wa, ba, wx, bx, lam, w_out):
    bsz, seq, d = x.shape
    d_in = w_in.shape[1]
    tm = TM_MIX
    wgate = jnp.concatenate([_block_diag_halves(wa), _block_diag_halves(wx)], axis=-1).astype(BF16)
    row = lambda a: a.reshape(1, -1)
    return pl.pallas_call(
        _mixer_kernel,
        grid=(bsz, seq // tm),
        in_specs=[
            pl.BlockSpec((None, tm, d), lambda b, j: (b, j, 0)),
            _const_spec((1, d)),
            _const_spec((d, d_in)),
            _const_spec(conv_w.shape),
            _const_spec(lru_conv_w.shape),
            _const_spec((1, W_LRU)),
            _const_spec(wgate.shape),
            _const_spec((1, W_LRU)),
            _const_spec((1, W_LRU)),
            _const_spec((1, W_LRU)),
            _const_spec(w_out.shape),
        ],
        out_specs=pl.BlockSpec((None, tm, d), lambda b, j: (b, j, 0)),
        out_shape=jax.ShapeDtypeStruct(x.shape, F32),
        scratch_shapes=[
            pltpu.VMEM((tm + 2 * SUBLANES, W_CONV), F32),
            pltpu.VMEM((tm + 2 * SUBLANES, W_LRU), F32),
            pltpu.VMEM((tm, W_LRU), F32),
            pltpu.VMEM((tm, W_LRU), F32),
            pltpu.VMEM((tm, W_LRU), F32),
            pltpu.VMEM((SUBLANES, W_LRU), F32),
        ],
        compiler_params=pltpu.CompilerParams(
            dimension_semantics=("arbitrary", "arbitrary"), vmem_limit_bytes=VMEM_LIMIT_BYTES),
        name="mixer",
    )(x, row(gn), w_in.astype(BF16), conv_w, lru_conv_w, row(lru_conv_b), wgate, row(ba), row(bx),
      row(lam), w_out.astype(BF16))


def _swiglu_rows(h, wg_ref, wu_ref, wd_ref):
    d_ff = wg_ref.shape[-1]
    acc = jnp.zeros((h.shape[0], wd_ref.shape[-1]), F32)
    for c in range(d_ff // FF_CHUNK):
        cols = slice(c * FF_CHUNK, (c + 1) * FF_CHUNK)
        gate = jnp.dot(h, wg_ref[:, cols], preferred_element_type=F32)
        up = jnp.dot(h, wu_ref[:, cols], preferred_element_type=F32)
        act = (gate * _sigmoid(gate) * up).astype(BF16)
        acc = acc + jnp.dot(act, wd_ref[cols, :], preferred_element_type=F32)
    return acc


def _ffn_kernel(x_ref, gn_ref, wg_ref, wu_ref, wd_ref, o_ref):
    x = x_ref[...]
    h = _rms_norm(x, gn_ref[...]).astype(BF16)
    o_ref[...] = x + _swiglu_rows(h, wg_ref, wu_ref, wd_ref)


def _ffn_dense(x2, gn, wg, wu, wd):
    t, d = x2.shape
    tm = TM_FFN
    return pl.pallas_call(
        _ffn_kernel,
        grid=(t // tm,),
        in_specs=[
            pl.BlockSpec((tm, d), lambda i: (i, 0)),
            _const_spec((1, d)),
            _const_spec(wg.shape),
            _const_spec(wu.shape),
            _const_spec(wd.shape),
        ],
        out_specs=pl.BlockSpec((tm, d), lambda i: (i, 0)),
        out_shape=jax.ShapeDtypeStruct(x2.shape, F32),
        compiler_params=pltpu.CompilerParams(
            dimension_semantics=("arbitrary",), vmem_limit_bytes=VMEM_LIMIT_BYTES),
        name="ffn_dense",
    )(x2, gn.reshape(1, -1), wg.astype(BF16), wu.astype(BF16), wd.astype(BF16))


def _router_kernel(x_ref, gn_ref, wr_ref, meta_ref, gates_ref, counts_ref, run):
    i = pl.program_id(0)
    tm = x_ref.shape[0]

    @pl.when(i == 0)
    def _():
        run[...] = jnp.zeros((1, LANES), F32)

    h = _rms_norm(x_ref[...], gn_ref[...])
    logits = jnp.dot(h, wr_ref[...], preferred_element_type=F32, precision=lax.Precision.HIGHEST)
    lane = lax.broadcasted_iota(I32, (tm, LANES), 1)
    neg_inf = jnp.float32(-jnp.inf)
    logits = jnp.where(lane < N_EXPERTS, logits, neg_inf)
    m1 = jnp.max(logits, axis=-1, keepdims=True)
    i1 = jnp.min(jnp.where(logits == m1, lane, LANES), axis=-1, keepdims=True)
    rest = jnp.where(lane == i1, neg_inf, logits)
    m2 = jnp.max(rest, axis=-1, keepdims=True)
    i2 = jnp.min(jnp.where(rest == m2, lane, LANES), axis=-1, keepdims=True)
    e = jnp.exp(m2 - m1)
    g1 = 1.0 / (1.0 + e)
    g2 = e * g1

    sel1 = lane == i1
    sel2 = lane == i2
    hot = jnp.where(sel1 | sel2, 1.0, 0.0)
    rows = lax.broadcasted_iota(I32, (tm, tm), 0)
    cols = lax.broadcasted_iota(I32, (tm, tm), 1)
    strict_lower = jnp.where(cols < rows, 1.0, 0.0).astype(BF16)
    before = jnp.dot(strict_lower, hot.astype(BF16), preferred_element_type=F32) + run[...]
    rank1 = jnp.sum(jnp.where(sel1, before, 0.0), axis=-1, keepdims=True)
    rank2 = jnp.sum(jnp.where(sel2, before, 0.0), axis=-1, keepdims=True)
    run[...] = run[...] + jnp.sum(hot, axis=0, keepdims=True)

    meta = jnp.where(lane == 0, i1, jnp.where(lane == 1, i2, jnp.where(
        lane == 2, rank1.astype(I32), jnp.where(lane == 3, rank2.astype(I32), 0))))
    meta_ref[...] = meta
    gates_ref[...] = jnp.where(lane == 0, g1, jnp.where(lane == 1, g2, 0.0))
    counts_ref[...] = jnp.broadcast_to(run[...], (SUBLANES, LANES))


def _router(x2, gn, w_router):
    t, d = x2.shape
    tm = TM_ROUTE
    wr = jnp.zeros((d, LANES), F32).at[:, :N_EXPERTS].set(w_router)
    return pl.pallas_call(
        _router_kernel,
        grid=(t // tm,),
        in_specs=[
            pl.BlockSpec((tm, d), lambda i: (i, 0)),
            _const_spec((1, d)),
            _const_spec((d, LANES)),
        ],
        out_specs=[
            pl.BlockSpec((tm, LANES), lambda i: (i, 0)),
            pl.BlockSpec((tm, LANES), lambda i: (i, 0)),
            pl.BlockSpec((SUBLANES, LANES), lambda i: (0, 0)),
        ],
        out_shape=[
            jax.ShapeDtypeStruct((t, LANES), I32),
            jax.ShapeDtypeStruct((t, LANES), F32),
            jax.ShapeDtypeStruct((SUBLANES, LANES), F32),
        ],
        scratch_shapes=[pltpu.VMEM((1, LANES), F32)],
        compiler_params=pltpu.CompilerParams(
            dimension_semantics=("arbitrary",), vmem_limit_bytes=VMEM_LIMIT_BYTES),
        name="router",
    )(x2, gn.reshape(1, -1), wr)


def _row_copy(src_ref, src_row, dst_ref, dst_row, sem):
    return pltpu.make_async_copy(src_ref.at[pl.ds(src_row, 1), :], dst_ref.at[pl.ds(dst_row, 1), :], sem)


def _dispatch_kernel(pad_ref, pos_ref, x_ref, xs_hbm, ztile, sem, zsem):
    i = pl.program_id(0)
    tm = x_ref.shape[0]
    tile_rows = ztile.shape[0]

    @pl.when(i == 0)
    def _():
        ztile[...] = jnp.zeros(ztile.shape, F32)
        for e in range(N_EXPERTS):
            start = pad_ref[0, e]
            n = pad_ref[1, e]

            def fill(q, c):
                _row_copy(ztile, 0, xs_hbm, start + q, zsem).start()
                return c

            def drain(q, c):
                _row_copy(ztile, 0, xs_hbm, start, zsem).wait()
                return c

            lax.fori_loop(0, n, fill, 0)
            lax.fori_loop(0, n, drain, 0)

        def tail_copy(q):
            r0 = pl.multiple_of(q * tile_rows, tile_rows)
            return pltpu.make_async_copy(ztile, xs_hbm.at[pl.ds(r0, tile_rows), :], zsem)

        def fill_tail(q, c):
            tail_copy(q).start()
            return c

        def drain_tail(q, c):
            tail_copy(q).wait()
            return c

        n_valid = pad_ref[2, 0]
        n_tiles = xs_hbm.shape[0] // tile_rows
        lax.fori_loop(n_valid, n_tiles, fill_tail, 0)
        lax.fori_loop(n_valid, n_tiles, drain_tail, 0)

    def issue(r, c):
        for k in range(TOP_K):
            _row_copy(x_ref, r, xs_hbm, pos_ref[0, 0, TOP_K * r + k], sem.at[k]).start(priority=k)
        return c

    lax.fori_loop(0, tm, issue, 0, unroll=ROW_DMA_UNROLL)
    for k in range(TOP_K):
        pltpu.make_async_copy(x_ref, xs_hbm.at[pl.ds(0, tm), :], sem.at[k]).wait()


def _dispatch(x2, pos, pad_info, n_rows):
    t, d = x2.shape
    tm = TM_DISP
    pos3 = pos.reshape(t // tm, 1, TOP_K * tm)
    grid_spec = pltpu.PrefetchScalarGridSpec(
        num_scalar_prefetch=1,
        grid=(t // tm,),
        in_specs=[
            pl.BlockSpec((1, 1, TOP_K * tm), lambda i, pad: (i, 0, 0), memory_space=pltpu.SMEM),
            pl.BlockSpec((tm, d), lambda i, pad: (i, 0)),
        ],
        out_specs=pl.BlockSpec(memory_space=pl.ANY),
        scratch_shapes=[
            pltpu.VMEM((TM_EXP, d), F32),
            pltpu.SemaphoreType.DMA((TOP_K,)),
            pltpu.SemaphoreType.DMA(()),
        ],
    )
    return pl.pallas_call(
        _dispatch_kernel,
        grid_spec=grid_spec,
        out_shape=jax.ShapeDtypeStruct((n_rows, d), F32),
        compiler_params=pltpu.CompilerParams(
            dimension_semantics=("arbitrary",), vmem_limit_bytes=VMEM_LIMIT_BYTES),
        name="dispatch",
    )(pad_info, pos3, x2)


def _expert_kernel(texp_ref, nvalid_ref, xs_ref, gn_ref, wg_ref, wu_ref, wd_ref, ys_ref):
    i = pl.program_id(0)

    @pl.when(i < nvalid_ref[0])
    def _():
        h = _rms_norm(xs_ref[...], gn_ref[...]).astype(BF16)
        ys_ref[...] = _swiglu_rows(h, wg_ref, wu_ref, wd_ref)

    @pl.when(i >= nvalid_ref[0])
    def _():
        ys_ref[...] = jnp.zeros(ys_ref.shape, F32)


def _experts(xs, gn, wg, wu, wd, tile_expert, n_valid):
    n_rows, d = xs.shape
    tm = TM_EXP
    d_ff = wg.shape[-1]
    tile = lambda i, te, nv: (jnp.minimum(i, nv[0] - 1), 0)
    expert = lambda i, te, nv: (te[jnp.minimum(i, nv[0] - 1)], 0, 0)
    grid_spec = pltpu.PrefetchScalarGridSpec(
        num_scalar_prefetch=2,
        grid=(n_rows // tm,),
        in_specs=[
            pl.BlockSpec((tm, d), tile),
            pl.BlockSpec((1, d), lambda i, te, nv: (0, 0)),
            pl.BlockSpec((None, d, d_ff), expert),
            pl.BlockSpec((None, d, d_ff), expert),
            pl.BlockSpec((None, d_ff, d), expert),
        ],
        out_specs=pl.BlockSpec((tm, d), lambda i, te, nv: (i, 0)),
    )
    return pl.pallas_call(
        _expert_kernel,
        grid_spec=grid_spec,
        out_shape=jax.ShapeDtypeStruct((n_rows, d), F32),
        compiler_params=pltpu.CompilerParams(
            dimension_semantics=("arbitrary",), vmem_limit_bytes=VMEM_LIMIT_BYTES),
        name="experts",
    )(tile_expert, n_valid, xs, gn.reshape(1, -1), wg.astype(BF16), wu.astype(BF16), wd.astype(BF16))


def _combine_kernel(pos_ref, x_ref, gates_ref, gf_ref, ys_hbm, o_ref, buf, sem):
    tm = x_ref.shape[0]

    def issue(r, c):
        for k in range(TOP_K):
            _row_copy(ys_hbm, pos_ref[0, 0, TOP_K * r + k], buf.at[k], r, sem.at[k]).start(priority=k)
        return c

    lax.fori_loop(0, tm, issue, 0, unroll=ROW_DMA_UNROLL)
    for k in range(TOP_K):
        pltpu.make_async_copy(ys_hbm.at[pl.ds(0, tm), :], buf.at[k], sem.at[k]).wait()
    gates = gates_ref[...]
    y = x_ref[...] + gates[:, 0:1] * buf[0] + gates[:, 1:2] * buf[1]
    o_ref[...] = _rms_norm(y, gf_ref[...])


def _combine(x2, gates, pos, ys, g_final):
    t, d = x2.shape
    tm = TM_DISP
    pos3 = pos.reshape(t // tm, 1, TOP_K * tm)
    return pl.pallas_call(
        _combine_kernel,
        grid=(t // tm,),
        in_specs=[
            pl.BlockSpec((1, 1, TOP_K * tm), lambda i: (i, 0, 0), memory_space=pltpu.SMEM),
            pl.BlockSpec((tm, d), lambda i: (i, 0)),
            pl.BlockSpec((tm, LANES), lambda i: (i, 0)),
            _const_spec((1, d)),
            pl.BlockSpec(memory_space=pl.ANY),
        ],
        out_specs=pl.BlockSpec((tm, d), lambda i: (i, 0)),
        out_shape=jax.ShapeDtypeStruct(x2.shape, F32),
        scratch_shapes=[pltpu.VMEM((TOP_K, tm, d), F32), pltpu.SemaphoreType.DMA((TOP_K,))],
        compiler_params=pltpu.CompilerParams(
            dimension_semantics=("arbitrary",), vmem_limit_bytes=VMEM_LIMIT_BYTES),
        name="combine",
    )(pos3, x2, gates, g_final.reshape(1, -1), ys)


def _moe_layer(x2, gn, w_router, wg, wu, wd, g_final):
    t, _ = x2.shape
    tm = TM_EXP
    n_tiles = (TOP_K * t) // tm + N_EXPERTS
    meta, gates, counts = _router(x2, gn, w_router)

    counts = counts[0, :N_EXPERTS].astype(I32)
    padded = ((counts + tm - 1) // tm) * tm
    ends = jnp.cumsum(padded)
    starts = ends - padded
    idx = meta[:, 0:TOP_K]
    rank = meta[:, TOP_K:2 * TOP_K]
    pos = (starts[idx] + rank).reshape(-1)
    tile_start = jnp.arange(n_tiles, dtype=I32) * tm
    tile_expert = jnp.minimum(
        jnp.sum((tile_start[:, None] >= ends[None, :]).astype(I32), axis=1), N_EXPERTS - 1).astype(I32)
    n_valid = (ends[-1:] // tm).astype(I32)
    pad_info = jnp.stack([starts + counts, padded - counts, jnp.broadcast_to(n_valid, (N_EXPERTS,))]).astype(I32)

    xs = _dispatch(x2, pos, pad_info, n_tiles * tm)
    ys = _experts(xs, gn, wg, wu, wd, tile_expert, n_valid)
    return _combine(x2, gates, pos, ys, g_final)


def kernel(x, norm_mix, norm_ffn, norm_final, w_in, conv_w, lru_conv_w, lru_conv_b, lru_wa, lru_ba, lru_wx,
           lru_bx, lru_lambda, w_out, ffn_w_gate, ffn_w_up, ffn_w_down, w_router, moe_w_gate, moe_w_up,
           moe_w_down):
    bsz, seq, d = x.shape
    depth = w_in.shape[0]
    assert depth == 2, "layer 0 uses the dense FFN, layer 1 the routed FFN followed by the final norm"
    for l in range(depth):
        x = _mixer(x, norm_mix[l], w_in[l], conv_w[l], lru_conv_w[l], lru_conv_b[l], lru_wa[l], lru_ba[l],
                   lru_wx[l], lru_bx[l], lru_lambda[l], w_out[l])
        x2 = x.reshape(bsz * seq, d)
        if l % 2 == 0:
            x = _ffn_dense(x2, norm_ffn[l], ffn_w_gate[l // 2], ffn_w_up[l // 2],
                           ffn_w_down[l // 2]).reshape(bsz, seq, d)
        else:
            x = _moe_layer(x2, norm_ffn[l], w_router[l // 2], moe_w_gate[l // 2], moe_w_up[l // 2],
                           moe_w_down[l // 2], norm_final).reshape(bsz, seq, d)
    return x
```

```python
import functools

import jax
import jax.numpy as jnp
from jax import lax
from jax.experimental import pallas as pl
from jax.experimental.pallas import tpu as pltpu

F32 = jnp.float32
BF16 = jnp.bfloat16
I32 = jnp.int32

EPS = 1e-6
RG_C = 8.0
W_CONV = 512
W_LRU = 512
N_EXPERTS = 8
TOP_K = 2

SUBLANES = 8
LANES = 128
MXU_DIM = 256
VMEM_LIMIT_BYTES = 56 * 1024 * 1024

TM_MIX = 256
TM_FFN = 512
FF_CHUNK = 512
TM_ROUTE = 512
TM_DISP = 512
ROW_DMA_UNROLL = 8
TM_EXP = 256


def _const_spec(shape):
    nd = len(shape)
    return pl.BlockSpec(shape, lambda *_: (0,) * nd, pipeline_mode=pl.Buffered(1))


def _rms_norm(x, g):
    ms = jnp.mean(x * x, axis=-1, keepdims=True)
    return x * lax.rsqrt(ms + EPS) * g


def _sigmoid(x):
    return 1.0 / (1.0 + jnp.exp(-x))


def _gelu_tanh(x):
    c = 0.7978845608028654
    return 0.5 * x * (1.0 + jnp.tanh(c * (x + 0.044715 * (x * x * x))))


def _tile_copies(hbm, tile, buf, slot, sem, to_hbm):
    seg = buf.shape[1]
    copies = []
    for i in range(SUBLANES):
        rows = hbm.at[pl.ds(tile * (seg * SUBLANES) + i * seg, seg), :]
        vm = buf.at[slot, :, i, :]
        copies.append(pltpu.make_async_copy(vm, rows, sem.at[slot]) if to_hbm
                      else pltpu.make_async_copy(rows, vm, sem.at[slot]))
    return copies


def _segment_halo(cur, prev, n):
    first = lax.broadcasted_iota(I32, (SUBLANES, cur.shape[1]), 0) == 0
    out = []
    for k in range(n):
        rows = slice(k * SUBLANES, (k + 1) * SUBLANES)
        out.append(jnp.where(first, pltpu.roll(prev[rows, :], 1, axis=0), pltpu.roll(cur[rows, :], 1, axis=0)))
    return jnp.concatenate(out, axis=0)


def _mixer_kernel(x_hbm, gn_ref, win_ref, cw_ref, lcw_ref, lcb_ref, wgate_ref, ba_ref, bx_ref, lam_ref,
                  wout_ref, o_hbm, xbuf, p_even, p_odd, obuf, ztail, utail, hcarry, xsem, osem, *,
                  n_tiles, tiles_per_seq):
    k = pl.program_id(0)
    last_step = pl.num_programs(0) - 1
    seg = xbuf.shape[1]
    tm = seg * SUBLANES
    d = xbuf.shape[3]
    n_x = xbuf.shape[0]
    t_even = 2 * k
    t_odd = 2 * k + 1

    def fetch(tile):
        return _tile_copies(x_hbm, tile, xbuf, tile % n_x, xsem, False)

    def write_back(tile, slot):
        return _tile_copies(o_hbm, tile, obuf, slot, osem, True)

    @pl.when(k == 0)
    def _():
        for c in fetch(0) + fetch(1):
            c.start()
        xbuf[n_x - 1] = jnp.zeros(xbuf.shape[1:], F32)
        p_odd[...] = jnp.zeros(p_odd.shape, F32)
        ztail[...] = jnp.zeros(ztail.shape, F32)
        utail[...] = jnp.zeros(utail.shape, F32)
        hcarry[...] = jnp.zeros(hcarry.shape, F32)

    @pl.when(k < last_step)
    def _():
        for c in fetch(t_even) + fetch(t_odd):
            c.wait()

    @pl.when(k + 1 < last_step)
    def _():
        for c in fetch(t_even + 2) + fetch(t_odd + 2):
            c.start()

    @pl.when(k >= 2)
    def _():
        for c in write_back(t_even - 3, 0):
            c.wait()

    @pl.when(k >= 1)
    def _():
        for c in write_back(t_even - 2, 1):
            c.wait()

    def project(tile, p_ref):
        xa = xbuf[jnp.minimum(tile, n_tiles - 1) % n_x].reshape(tm, d)
        ha = _rms_norm(xa, gn_ref[...]).astype(BF16)
        p_ref[...] = jnp.dot(ha, win_ref[...], preferred_element_type=F32)

    def finish(tile, p_ref, state):
        z_tail, u_tail, h_last = state
        fresh = (tile + tiles_per_seq) % tiles_per_seq == 0
        z_tail = jnp.where(fresh, 0.0, z_tail)
        u_tail = jnp.where(fresh, 0.0, u_tail)
        h_last = jnp.where(fresh, 0.0, h_last)
        c_gate = p_ref[:, 0:W_CONV]
        b_gate = p_ref[:, W_CONV:2 * W_CONV]
        v = p_ref[:, 2 * W_CONV:3 * W_CONV]
        u = p_ref[:, 3 * W_CONV:3 * W_CONV + W_LRU]
        g = p_ref[:, 3 * W_CONV + W_LRU:]

        z = c_gate * v
        nz = z_tail.shape[0]
        z_new_tail = z[tm - nz:, :]
        zx = jnp.concatenate([_segment_halo(z_new_tail, z_tail, nz // SUBLANES), z], axis=0)
        cw = cw_ref[...]
        conv = cw[2:3, :] * z + cw[1:2, :] * zx[nz - SUBLANES:nz - SUBLANES + tm, :] + cw[0:1, :] * zx[0:tm, :]
        y_a = b_gate * conv

        nu = u_tail.shape[0]
        u_new_tail = u[tm - nu:, :]
        ux = jnp.concatenate([_segment_halo(u_new_tail, u_tail, nu // SUBLANES), u], axis=0)
        lw = lcw_ref[...]
        uc = (lw[3:4, :] * u + lw[2:3, :] * ux[2 * SUBLANES:2 * SUBLANES + tm, :]
              + lw[1:2, :] * ux[SUBLANES:SUBLANES + tm, :] + lw[0:1, :] * ux[0:tm, :] + lcb_ref[...])

        ucb = uc.astype(BF16)
        pre0 = jnp.dot(ucb[:, 0:MXU_DIM], wgate_ref[0], preferred_element_type=F32)
        pre1 = jnp.dot(ucb[:, MXU_DIM:], wgate_ref[1], preferred_element_type=F32)
        pre_a = jnp.concatenate([pre0[:, :MXU_DIM], pre1[:, :MXU_DIM]], axis=1) + ba_ref[...]
        pre_x = jnp.concatenate([pre0[:, MXU_DIM:], pre1[:, MXU_DIM:]], axis=1) + bx_ref[...]
        r = _sigmoid(pre_a)
        i_gate = _sigmoid(pre_x)
        neg_lam = -lam_ref[...]
        softplus = jnp.maximum(neg_lam, 0.0) + jnp.log1p(jnp.exp(-jnp.abs(neg_lam)))
        log_a = (-RG_C) * r * softplus
        a = jnp.exp(log_a)
        th = jnp.tanh(log_a)
        b = jnp.sqrt((-2.0 * th) / (1.0 - th)) * (i_gate * uc)

        a3 = a.reshape(seg, SUBLANES, W_LRU)
        b3 = b.reshape(seg, SUBLANES, W_LRU)
        prod = [a3[0]]
        part = [b3[0]]
        for j in range(1, seg):
            prod.append(a3[j] * prod[j - 1])
            part.append(a3[j] * part[j - 1] + b3[j])
        init_rows = []
        for i in range(SUBLANES):
            init_rows.append(h_last)
            h_last = prod[seg - 1][i:i + 1, :] * h_last + part[seg - 1][i:i + 1, :]
        init = jnp.concatenate(init_rows, axis=0)
        hs = jnp.stack([part[j] + prod[j] * init for j in range(seg)], axis=0).reshape(tm, W_LRU)

        y_b = hs * _gelu_tanh(g)
        y = jnp.concatenate([y_a, y_b], axis=1).astype(BF16)
        xc = xbuf[(tile + n_x) % n_x].reshape(tm, d)
        out = xc + jnp.dot(y, wout_ref[...], preferred_element_type=F32)
        return out.reshape(seg, SUBLANES, d), (z_new_tail, u_new_tail, h_last)

    state = (ztail[...], utail[...], hcarry[0:1, :])
    project(t_even, p_even)
    obuf[0], state = finish(t_even - 1, p_odd, state)
    project(t_odd, p_odd)
    obuf[1], state = finish(t_even, p_even, state)
    ztail[...], utail[...] = state[0], state[1]
    hcarry[...] = jnp.broadcast_to(state[2], hcarry.shape)

    @pl.when(k >= 1)
    def _():
        for c in write_back(t_even - 1, 0):
            c.start()

    @pl.when(k < last_step)
    def _():
        for c in write_back(t_even, 1):
            c.start()

    @pl.when(k == last_step)
    def _():
        for c in write_back(t_even - 1, 0):
            c.wait()


def _block_diag_halves(w):
    nh, dh, _ = w.shape
    per = MXU_DIM // dh
    w4 = w.reshape(nh // per, per, dh, dh)
    eye = jnp.eye(per, dtype=w.dtype)
    return jnp.einsum("kaij,ab->kaibj", w4, eye).reshape(nh // per, MXU_DIM, MXU_DIM)


def _mixer(x, gn, w_in, conv_w, lru_conv_w, lru_conv_b, wa, ba, wx, bx, lam, w_out):
    bsz, seq, d = x.shape
    d_in = w_in.shape[1]
    tm = TM_MIX
    seg = tm // SUBLANES
    n_tiles = (bsz * seq) // tm
    assert n_tiles % 2 == 0 and (seq // tm) % 2 == 0
    wgate = jnp.concatenate([_block_diag_halves(wa), _block_diag_halves(wx)], axis=-1).astype(BF16)
    row = lambda a: a.reshape(1, -1)
    out = pl.pallas_call(
        functools.partial(_mixer_kernel, n_tiles=n_tiles, tiles_per_seq=seq // tm),
        grid=(n_tiles // 2 + 1,),
        in_specs=[
            pl.BlockSpec(memory_space=pl.ANY),
            _const_spec((1, d)),
            _const_spec((d, d_in)),
            _const_spec(conv_w.shape),
            _const_spec(lru_conv_w.shape),
            _const_spec((1, W_LRU)),
            _const_spec(wgate.shape),
            _const_spec((1, W_LRU)),
            _const_spec((1, W_LRU)),
            _const_spec((1, W_LRU)),
            _const_spec(w_out.shape),
        ],
        out_specs=pl.BlockSpec(memory_space=pl.ANY),
        out_shape=jax.ShapeDtypeStruct((bsz * seq, d), F32),
        scratch_shapes=[
            pltpu.VMEM((6, seg, SUBLANES, d), F32),
            pltpu.VMEM((tm, d_in), F32),
            pltpu.VMEM((tm, d_in), F32),
            pltpu.VMEM((2, seg, SUBLANES, d), F32),
            pltpu.VMEM(((conv_w.shape[0] - 1) * SUBLANES, W_CONV), F32),
            pltpu.VMEM(((lru_conv_w.shape[0] - 1) * SUBLANES, W_LRU), F32),
            pltpu.VMEM((SUBLANES, W_LRU), F32),
            pltpu.SemaphoreType.DMA((6,)),
            pltpu.SemaphoreType.DMA((2,)),
        ],
        compiler_params=pltpu.CompilerParams(
            dimension_semantics=("arbitrary",), vmem_limit_bytes=VMEM_LIMIT_BYTES),
        name="mixer",
    )(x.reshape(bsz * seq, d), row(gn), w_in.astype(BF16), conv_w, lru_conv_w, row(lru_conv_b), wgate,
      row(ba), row(bx), row(lam), w_out.astype(BF16))
    return out.reshape(bsz, seq, d)


def _swiglu_rows(h, wg_ref, wu_ref, wd_ref):
    d_ff = wg_ref.shape[-1]
    acc = jnp.zeros((h.shape[0], wd_ref.shape[-1]), F32)
    for c in range(d_ff // FF_CHUNK):
        cols = slice(c * FF_CHUNK, (c + 1) * FF_CHUNK)
        gate = jnp.dot(h, wg_ref[:, cols], preferred_element_type=F32)
        up = jnp.dot(h, wu_ref[:, cols], preferred_element_type=F32)
        act = (gate * _sigmoid(gate) * up).astype(BF16)
        acc = acc + jnp.dot(act, wd_ref[cols, :], preferred_element_type=F32)
    return acc


def _ffn_kernel(x_ref, gn_ref, wg_ref, wu_ref, wd_ref, o_ref):
    x = x_ref[...]
    h = _rms_norm(x, gn_ref[...]).astype(BF16)
    o_ref[...] = x + _swiglu_rows(h, wg_ref, wu_ref, wd_ref)


def _ffn_dense(x2, gn, wg, wu, wd):
    t, d = x2.shape
    tm = TM_FFN
    return pl.pallas_call(
        _ffn_kernel,
        grid=(t // tm,),
        in_specs=[
            pl.BlockSpec((tm, d), lambda i: (i, 0)),
            _const_spec((1, d)),
            _const_spec(wg.shape),
            _const_spec(wu.shape),
            _const_spec(wd.shape),
        ],
        out_specs=pl.BlockSpec((tm, d), lambda i: (i, 0)),
        out_shape=jax.ShapeDtypeStruct(x2.shape, F32),
        compiler_params=pltpu.CompilerParams(
            dimension_semantics=("arbitrary",), vmem_limit_bytes=VMEM_LIMIT_BYTES),
        name="ffn_dense",
    )(x2, gn.reshape(1, -1), wg.astype(BF16), wu.astype(BF16), wd.astype(BF16))


def _split_bf16(a):
    hi = a.astype(BF16)
    return hi, (a - hi.astype(F32)).astype(BF16)


def _router_kernel(x_ref, gn_ref, wr_hi_ref, wr_lo_ref, meta_ref, gates_ref, counts_ref, run):
    i = pl.program_id(0)
    tm = x_ref.shape[0]

    @pl.when(i == 0)
    def _():
        run[...] = jnp.zeros((1, LANES), F32)

    h_hi, h_lo = _split_bf16(_rms_norm(x_ref[...], gn_ref[...]))
    logits = (jnp.dot(h_hi, wr_hi_ref[...], preferred_element_type=F32)
              + jnp.dot(h_lo, wr_hi_ref[...], preferred_element_type=F32)
              + jnp.dot(h_hi, wr_lo_ref[...], preferred_element_type=F32))
    lane = lax.broadcasted_iota(I32, (tm, LANES), 1)
    neg_inf = jnp.float32(-jnp.inf)
    logits = jnp.where(lane < N_EXPERTS, logits, neg_inf)
    m1 = jnp.max(logits, axis=-1, keepdims=True)
    i1 = jnp.min(jnp.where(logits == m1, lane, LANES), axis=-1, keepdims=True)
    rest = jnp.where(lane == i1, neg_inf, logits)
    m2 = jnp.max(rest, axis=-1, keepdims=True)
    i2 = jnp.min(jnp.where(rest == m2, lane, LANES), axis=-1, keepdims=True)
    e = jnp.exp(m2 - m1)
    g1 = 1.0 / (1.0 + e)
    g2 = e * g1

    sel1 = lane == i1
    sel2 = lane == i2
    hot = jnp.where(sel1 | sel2, 1.0, 0.0)
    rows = lax.broadcasted_iota(I32, (tm, tm), 0)
    cols = lax.broadcasted_iota(I32, (tm, tm), 1)
    strict_lower = jnp.where(cols < rows, 1.0, 0.0).astype(BF16)
    before = jnp.dot(strict_lower, hot.astype(BF16), preferred_element_type=F32) + run[...]
    rank1 = jnp.sum(jnp.where(sel1, before, 0.0), axis=-1, keepdims=True)
    rank2 = jnp.sum(jnp.where(sel2, before, 0.0), axis=-1, keepdims=True)
    run[...] = run[...] + jnp.sum(hot, axis=0, keepdims=True)

    meta = jnp.where(lane == 0, i1, jnp.where(lane == 1, i2, jnp.where(
        lane == 2, rank1.astype(I32), jnp.where(lane == 3, rank2.astype(I32), 0))))
    meta_ref[...] = meta
    gates_ref[...] = jnp.where(lane == 0, g1, jnp.where(lane == 1, g2, 0.0))
    counts_ref[...] = jnp.broadcast_to(run[...], (SUBLANES, LANES))


def _router(x2, gn, w_router):
    t, d = x2.shape
    tm = TM_ROUTE
    wr_hi, wr_lo = _split_bf16(jnp.zeros((d, LANES), F32).at[:, :N_EXPERTS].set(w_router))
    return pl.pallas_call(
        _router_kernel,
        grid=(t // tm,),
        in_specs=[
            pl.BlockSpec((tm, d), lambda i: (i, 0)),
            _const_spec((1, d)),
            _const_spec((d, LANES)),
            _const_spec((d, LANES)),
        ],
        out_specs=[
            pl.BlockSpec((tm, LANES), lambda i: (i, 0)),
            pl.BlockSpec((tm, LANES), lambda i: (i, 0)),
            pl.BlockSpec((SUBLANES, LANES), lambda i: (0, 0)),
        ],
        out_shape=[
            jax.ShapeDtypeStruct((t, LANES), I32),
            jax.ShapeDtypeStruct((t, LANES), F32),
            jax.ShapeDtypeStruct((SUBLANES, LANES), F32),
        ],
        scratch_shapes=[pltpu.VMEM((1, LANES), F32)],
        compiler_params=pltpu.CompilerParams(
            dimension_semantics=("arbitrary",), vmem_limit_bytes=VMEM_LIMIT_BYTES),
        name="router",
    )(x2, gn.reshape(1, -1), wr_hi, wr_lo)


def _row_copy(src_ref, src_row, dst_ref, dst_row, sem):
    return pltpu.make_async_copy(src_ref.at[pl.ds(src_row, 1), :], dst_ref.at[pl.ds(dst_row, 1), :], sem)


def _dispatch_kernel(pad_ref, pos_ref, x_ref, xs_hbm, ztile, sem, zsem):
    i = pl.program_id(0)
    tile_rows = ztile.shape[0]

    @pl.when(i == 0)
    def _():
        ztile[...] = jnp.zeros(ztile.shape, F32)
        for e in range(N_EXPERTS):
            start = pad_ref[0, e]
            n = pad_ref[1, e]

            def fill(q, c):
                _row_copy(ztile, 0, xs_hbm, start + q, zsem).start()
                return c

            def drain(q, c):
                _row_copy(ztile, 0, xs_hbm, start, zsem).wait()
                return c

            lax.fori_loop(0, n, fill, 0)
            lax.fori_loop(0, n, drain, 0)

        def tail_copy(q):
            r0 = pl.multiple_of(q * tile_rows, tile_rows)
            return pltpu.make_async_copy(ztile, xs_hbm.at[pl.ds(r0, tile_rows), :], zsem)

        def fill_tail(q, c):
            tail_copy(q).start()
            return c

        def drain_tail(q, c):
            tail_copy(q).wait()
            return c

        n_valid = pad_ref[2, 0]
        n_tiles = xs_hbm.shape[0] // tile_rows
        lax.fori_loop(n_valid, n_tiles, fill_tail, 0)
        lax.fori_loop(n_valid, n_tiles, drain_tail, 0)

    def issue(q, c):
        for s in range(SUBLANES):
            for k in range(TOP_K):
                dst = pos_ref[0, 0, TOP_K * (q * SUBLANES + s) + k]
                pltpu.make_async_copy(x_ref.at[q, pl.ds(s, 1), :], xs_hbm.at[pl.ds(dst, 1), :],
                                      sem.at[k]).start(priority=k)
        return c

    n_groups = x_ref.shape[0]
    lax.fori_loop(0, n_groups, issue, 0)
    for k in range(TOP_K):
        for _ in range(n_groups * SUBLANES // tile_rows):
            pltpu.make_async_copy(ztile, xs_hbm.at[pl.ds(0, tile_rows), :], sem.at[k]).wait()


def _dispatch(x2, pos, pad_info, n_rows):
    t, d = x2.shape
    tm = TM_DISP
    pos3 = pos.reshape(t // tm, 1, TOP_K * tm)
    grid_spec = pltpu.PrefetchScalarGridSpec(
        num_scalar_prefetch=1,
        grid=(t // tm,),
        in_specs=[
            pl.BlockSpec((1, 1, TOP_K * tm), lambda i, pad: (i, 0, 0), memory_space=pltpu.SMEM),
            pl.BlockSpec((tm // SUBLANES, SUBLANES, d), lambda i, pad: (i, 0, 0)),
        ],
        out_specs=pl.BlockSpec(memory_space=pl.ANY),
        scratch_shapes=[
            pltpu.VMEM((TM_EXP, d), F32),
            pltpu.SemaphoreType.DMA((TOP_K,)),
            pltpu.SemaphoreType.DMA(()),
        ],
    )
    return pl.pallas_call(
        _dispatch_kernel,
        grid_spec=grid_spec,
        out_shape=jax.ShapeDtypeStruct((n_rows, d), F32),
        compiler_params=pltpu.CompilerParams(
            dimension_semantics=("arbitrary",), vmem_limit_bytes=VMEM_LIMIT_BYTES),
        name="dispatch",
    )(pad_info, pos3, x2.reshape(t // SUBLANES, SUBLANES, d))


def _expert_kernel(texp_ref, nvalid_ref, xs_ref, gn_ref, wg_ref, wu_ref, wd_ref, ys_ref):
    i = pl.program_id(0)

    @pl.when(i < nvalid_ref[0])
    def _():
        h = _rms_norm(xs_ref[...], gn_ref[...]).astype(BF16)
        ys_ref[...] = _swiglu_rows(h, wg_ref, wu_ref, wd_ref)

    @pl.when(i >= nvalid_ref[0])
    def _():
        ys_ref[...] = jnp.zeros(ys_ref.shape, F32)


def _experts(xs, gn, wg, wu, wd, tile_expert, n_valid):
    n_rows, d = xs.shape
    tm = TM_EXP
    d_ff = wg.shape[-1]
    last = lambda i, nv: jnp.maximum(jnp.minimum(i, nv[0] - 1), 0)
    tile = lambda i, te, nv: (last(i, nv), 0)
    expert = lambda i, te, nv: (te[last(i, nv)], 0, 0)
    grid_spec = pltpu.PrefetchScalarGridSpec(
        num_scalar_prefetch=2,
        grid=(n_rows // tm,),
        in_specs=[
            pl.BlockSpec((tm, d), tile),
            pl.BlockSpec((1, d), lambda i, te, nv: (0, 0)),
            pl.BlockSpec((None, d, d_ff), expert),
            pl.BlockSpec((None, d, d_ff), expert),
            pl.BlockSpec((None, d_ff, d), expert),
        ],
        out_specs=pl.BlockSpec((tm, d), lambda i, te, nv: (i, 0)),
    )
    return pl.pallas_call(
        _expert_kernel,
        grid_spec=grid_spec,
        out_shape=jax.ShapeDtypeStruct((n_rows, d), F32),
        compiler_params=pltpu.CompilerParams(
            dimension_semantics=("arbitrary",), vmem_limit_bytes=VMEM_LIMIT_BYTES),
        name="experts",
    )(tile_expert, n_valid, xs, gn.reshape(1, -1), wg.astype(BF16), wu.astype(BF16), wd.astype(BF16))


def _combine_kernel(pos_ref, x_ref, gates_ref, gf_ref, ys_hbm, o_ref, buf, sem):
    tm = x_ref.shape[0]

    def issue(r, c):
        for k in range(TOP_K):
            _row_copy(ys_hbm, pos_ref[0, 0, TOP_K * r + k], buf.at[k], r, sem.at[k]).start(priority=k)
        return c

    lax.fori_loop(0, tm, issue, 0, unroll=ROW_DMA_UNROLL)
    for k in range(TOP_K):
        pltpu.make_async_copy(ys_hbm.at[pl.ds(0, tm), :], buf.at[k], sem.at[k]).wait()
    gates = gates_ref[...]
    y = x_ref[...] + gates[:, 0:1] * buf[0] + gates[:, 1:2] * buf[1]
    o_ref[...] = _rms_norm(y, gf_ref[...])


def _combine(x2, gates, pos, ys, g_final):
    t, d = x2.shape
    tm = TM_DISP
    pos3 = pos.reshape(t // tm, 1, TOP_K * tm)
    return pl.pallas_call(
        _combine_kernel,
        grid=(t // tm,),
        in_specs=[
            pl.BlockSpec((1, 1, TOP_K * tm), lambda i: (i, 0, 0), memory_space=pltpu.SMEM),
            pl.BlockSpec((tm, d), lambda i: (i, 0)),
            pl.BlockSpec((tm, LANES), lambda i: (i, 0)),
            _const_spec((1, d)),
            pl.BlockSpec(memory_space=pl.ANY),
        ],
        out_specs=pl.BlockSpec((tm, d), lambda i: (i, 0)),
        out_shape=jax.ShapeDtypeStruct(x2.shape, F32),
        scratch_shapes=[pltpu.VMEM((TOP_K, tm, d), F32), pltpu.SemaphoreType.DMA((TOP_K,))],
        compiler_params=pltpu.CompilerParams(
            dimension_semantics=("arbitrary",), vmem_limit_bytes=VMEM_LIMIT_BYTES),
        name="combine",
    )(pos3, x2, gates, g_final.reshape(1, -1), ys)


def _moe_layer(x2, gn, w_router, wg, wu, wd, g_final):
    t, _ = x2.shape
    tm = TM_EXP
    n_tiles = (TOP_K * t) // tm + N_EXPERTS
    meta, gates, counts = _router(x2, gn, w_router)

    counts = counts[0, :N_EXPERTS].astype(I32)
    padded = ((counts + tm - 1) // tm) * tm
    ends = jnp.cumsum(padded)
    starts = ends - padded
    idx = meta[:, 0:TOP_K]
    rank = meta[:, TOP_K:2 * TOP_K]
    pos = (starts[idx] + rank).reshape(-1)
    tile_start = jnp.arange(n_tiles, dtype=I32) * tm
    tile_expert = jnp.minimum(
        jnp.sum((tile_start[:, None] >= ends[None, :]).astype(I32), axis=1), N_EXPERTS - 1).astype(I32)
    n_valid = (ends[-1:] // tm).astype(I32)
    pad_info = jnp.stack([starts + counts, padded - counts, jnp.broadcast_to(n_valid, (N_EXPERTS,))]).astype(I32)

    xs = _dispatch(x2, pos, pad_info, n_tiles * tm)
    ys = _experts(xs, gn, wg, wu, wd, tile_expert, n_valid)
    return _combine(x2, gates, pos, ys, g_final)


def kernel(x, norm_mix, norm_ffn, norm_final, w_in, conv_w, lru_conv_w, lru_conv_b, lru_wa, lru_ba, lru_wx,
           lru_bx, lru_lambda, w_out, ffn_w_gate, ffn_w_up, ffn_w_down, w_router, moe_w_gate, moe_w_up,
           moe_w_down):
    bsz, seq, d = x.shape
    depth = w_in.shape[0]
    assert depth == 2, "layer 0 uses the dense FFN, layer 1 the routed FFN followed by the final norm"
    for l in range(depth):
        x = _mixer(x, norm_mix[l], w_in[l], conv_w[l], lru_conv_w[l], lru_conv_b[l], lru_wa[l], lru_ba[l],
                   lru_wx[l], lru_bx[l], lru_lambda[l], w_out[l])
        x2 = x.reshape(bsz * seq, d)
        if l % 2 == 0:
            x = _ffn_dense(x2, norm_ffn[l], ffn_w_gate[l // 2], ffn_w_up[l // 2],
                           ffn_w_down[l // 2]).reshape(bsz, seq, d)
        else:
            x = _moe_layer(x2, norm_ffn[l], w_router[l // 2], moe_w_gate[l // 2], moe_w_up[l // 2],
                           moe_w_down[l // 2], norm_final).reshape(bsz, seq, d)
    return x
```

```python
import functools

import jax
import jax.numpy as jnp
from jax import lax
from jax.experimental import pallas as pl
from jax.experimental.pallas import tpu as pltpu

F32 = jnp.float32
BF16 = jnp.bfloat16
I32 = jnp.int32

EPS = 1e-6
RG_C = 8.0
W_CONV = 512
W_LRU = 512
N_EXPERTS = 8
TOP_K = 2

SUBLANES = 8
LANES = 128
MXU_DIM = 256
VMEM_LIMIT_BYTES = 56 * 1024 * 1024

TM_MIX = 256
TM_FFN = 512
FF_CHUNK = 512
TM_ROUTE = 512
TM_DISP = 512
ROW_DMA_UNROLL = 8
TM_EXP = 512


def _const_spec(shape):
    nd = len(shape)
    return pl.BlockSpec(shape, lambda *_: (0,) * nd, pipeline_mode=pl.Buffered(1))


def _rms_norm(x, g):
    ms = jnp.mean(x * x, axis=-1, keepdims=True)
    return x * lax.rsqrt(ms + EPS) * g


def _sigmoid(x):
    return 1.0 / (1.0 + jnp.exp(-x))


def _gelu_tanh(x):
    c = 0.7978845608028654
    return 0.5 * x * (1.0 + jnp.tanh(c * (x + 0.044715 * (x * x * x))))


def _segment_major_fetch(hbm, tile, buf, slot, sem):
    seg = buf.shape[1]
    return [pltpu.make_async_copy(hbm.at[pl.ds(tile * (seg * SUBLANES) + i * seg, seg), :],
                                  buf.at[slot, :, i, :], sem.at[slot]) for i in range(SUBLANES)]


def _from_segment_major(a, tm):
    rows, d = a.shape
    seg = tm // SUBLANES
    return a.reshape(rows // tm, seg, SUBLANES, d).swapaxes(1, 2).reshape(rows, d)


def _segment_halo(cur, prev, n):
    first = lax.broadcasted_iota(I32, (SUBLANES, cur.shape[1]), 0) == 0
    out = []
    for k in range(n):
        rows = slice(k * SUBLANES, (k + 1) * SUBLANES)
        out.append(jnp.where(first, pltpu.roll(prev[rows, :], 1, axis=0), pltpu.roll(cur[rows, :], 1, axis=0)))
    return jnp.concatenate(out, axis=0)


def _mixer_kernel(x_hbm, gn_ref, win_ref, cw_ref, lcw_ref, lcb_ref, wgate_ref, ba_ref, bx_ref, lam_ref,
                  wout_ref, o_hbm, xbuf, p_even, p_odd, obuf, ztail, utail, hcarry, xsem, osem, *,
                  n_tiles, tiles_per_seq, time_ordered_input):
    k = pl.program_id(0)
    last_step = pl.num_programs(0) - 1
    seg = xbuf.shape[1]
    tm = seg * SUBLANES
    d = xbuf.shape[3]
    n_x = xbuf.shape[0]
    t_even = 2 * k
    t_odd = 2 * k + 1

    def fetch(tile):
        slot = tile % n_x
        if time_ordered_input:
            return _segment_major_fetch(x_hbm, tile, xbuf, slot, xsem)
        return [pltpu.make_async_copy(x_hbm.at[pl.ds(tile * seg, seg)], xbuf.at[slot], xsem.at[slot])]

    def write_back(tile, slot):
        return [pltpu.make_async_copy(obuf.at[slot], o_hbm.at[pl.ds(tile * seg, seg)], osem.at[slot])]

    @pl.when(k == 0)
    def _():
        for c in fetch(0) + fetch(1):
            c.start()
        xbuf[n_x - 1] = jnp.zeros(xbuf.shape[1:], F32)
        p_odd[...] = jnp.zeros(p_odd.shape, F32)
        ztail[...] = jnp.zeros(ztail.shape, F32)
        utail[...] = jnp.zeros(utail.shape, F32)
        hcarry[...] = jnp.zeros(hcarry.shape, F32)

    @pl.when(k < last_step)
    def _():
        for c in fetch(t_even) + fetch(t_odd):
            c.wait()

    @pl.when(k + 1 < last_step)
    def _():
        for c in fetch(t_even + 2) + fetch(t_odd + 2):
            c.start()

    @pl.when(k >= 2)
    def _():
        for c in write_back(t_even - 3, 0):
            c.wait()

    @pl.when(k >= 1)
    def _():
        for c in write_back(t_even - 2, 1):
            c.wait()

    def project(tile, p_ref):
        xa = xbuf[jnp.minimum(tile, n_tiles - 1) % n_x].reshape(tm, d)
        ha = _rms_norm(xa, gn_ref[...]).astype(BF16)
        p_ref[...] = jnp.dot(ha, win_ref[...], preferred_element_type=F32)

    def finish(tile, p_ref, state):
        z_tail, u_tail, h_last = state
        fresh = (tile + tiles_per_seq) % tiles_per_seq == 0
        z_tail = jnp.where(fresh, 0.0, z_tail)
        u_tail = jnp.where(fresh, 0.0, u_tail)
        h_last = jnp.where(fresh, 0.0, h_last)
        c_gate = p_ref[:, 0:W_CONV]
        b_gate = p_ref[:, W_CONV:2 * W_CONV]
        v = p_ref[:, 2 * W_CONV:3 * W_CONV]
        u = p_ref[:, 3 * W_CONV:3 * W_CONV + W_LRU]
        g = p_ref[:, 3 * W_CONV + W_LRU:]

        z = c_gate * v
        nz = z_tail.shape[0]
        z_new_tail = z[tm - nz:, :]
        zx = jnp.concatenate([_segment_halo(z_new_tail, z_tail, nz // SUBLANES), z], axis=0)
        cw = cw_ref[...]
        conv = cw[2:3, :] * z + cw[1:2, :] * zx[nz - SUBLANES:nz - SUBLANES + tm, :] + cw[0:1, :] * zx[0:tm, :]
        y_a = b_gate * conv

        nu = u_tail.shape[0]
        u_new_tail = u[tm - nu:, :]
        ux = jnp.concatenate([_segment_halo(u_new_tail, u_tail, nu // SUBLANES), u], axis=0)
        lw = lcw_ref[...]
        uc = (lw[3:4, :] * u + lw[2:3, :] * ux[2 * SUBLANES:2 * SUBLANES + tm, :]
              + lw[1:2, :] * ux[SUBLANES:SUBLANES + tm, :] + lw[0:1, :] * ux[0:tm, :] + lcb_ref[...])

        ucb = uc.astype(BF16)
        pre0 = jnp.dot(ucb[:, 0:MXU_DIM], wgate_ref[0], preferred_element_type=F32)
        pre1 = jnp.dot(ucb[:, MXU_DIM:], wgate_ref[1], preferred_element_type=F32)
        pre_a = jnp.concatenate([pre0[:, :MXU_DIM], pre1[:, :MXU_DIM]], axis=1) + ba_ref[...]
        pre_x = jnp.concatenate([pre0[:, MXU_DIM:], pre1[:, MXU_DIM:]], axis=1) + bx_ref[...]
        r = _sigmoid(pre_a)
        i_gate = _sigmoid(pre_x)
        neg_lam = -lam_ref[...]
        softplus = jnp.maximum(neg_lam, 0.0) + jnp.log1p(jnp.exp(-jnp.abs(neg_lam)))
        log_a = (-RG_C) * r * softplus
        a = jnp.exp(log_a)
        th = jnp.tanh(log_a)
        b = jnp.sqrt((-2.0 * th) / (1.0 - th)) * (i_gate * uc)

        a3 = a.reshape(seg, SUBLANES, W_LRU)
        b3 = b.reshape(seg, SUBLANES, W_LRU)
        prod = [a3[0]]
        part = [b3[0]]
        for j in range(1, seg):
            prod.append(a3[j] * prod[j - 1])
            part.append(a3[j] * part[j - 1] + b3[j])
        init_rows = []
        for i in range(SUBLANES):
            init_rows.append(h_last)
            h_last = prod[seg - 1][i:i + 1, :] * h_last + part[seg - 1][i:i + 1, :]
        init = jnp.concatenate(init_rows, axis=0)
        hs = jnp.stack([part[j] + prod[j] * init for j in range(seg)], axis=0).reshape(tm, W_LRU)

        y_b = hs * _gelu_tanh(g)
        y = jnp.concatenate([y_a, y_b], axis=1).astype(BF16)
        xc = xbuf[(tile + n_x) % n_x].reshape(tm, d)
        out = xc + jnp.dot(y, wout_ref[...], preferred_element_type=F32)
        return out.reshape(seg, SUBLANES, d), (z_new_tail, u_new_tail, h_last)

    state = (ztail[...], utail[...], hcarry[0:1, :])
    project(t_even, p_even)
    obuf[0], state = finish(t_even - 1, p_odd, state)
    project(t_odd, p_odd)
    obuf[1], state = finish(t_even, p_even, state)
    ztail[...], utail[...] = state[0], state[1]
    hcarry[...] = jnp.broadcast_to(state[2], hcarry.shape)

    @pl.when(k >= 1)
    def _():
        for c in write_back(t_even - 1, 0):
            c.start()

    @pl.when(k < last_step)
    def _():
        for c in write_back(t_even, 1):
            c.start()

    @pl.when(k == last_step)
    def _():
        for c in write_back(t_even - 1, 0):
            c.wait()


def _block_diag_halves(w):
    nh, dh, _ = w.shape
    per = MXU_DIM // dh
    w4 = w.reshape(nh // per, per, dh, dh)
    eye = jnp.eye(per, dtype=w.dtype)
    return jnp.einsum("kaij,ab->kaibj", w4, eye).reshape(nh // per, MXU_DIM, MXU_DIM)


def _mixer(x2, seq, time_ordered_input, gn, w_in, conv_w, lru_conv_w, lru_conv_b, wa, ba, wx, bx, lam, w_out):
    t, d = x2.shape
    d_in = w_in.shape[1]
    tm = TM_MIX
    seg = tm // SUBLANES
    n_tiles = t // tm
    assert n_tiles % 2 == 0 and (seq // tm) % 2 == 0
    wgate = jnp.concatenate([_block_diag_halves(wa), _block_diag_halves(wx)], axis=-1).astype(BF16)
    row = lambda a: a.reshape(1, -1)
    x_in = x2 if time_ordered_input else x2.reshape(t // SUBLANES, SUBLANES, d)
    out = pl.pallas_call(
        functools.partial(_mixer_kernel, n_tiles=n_tiles, tiles_per_seq=seq // tm,
                          time_ordered_input=time_ordered_input),
        grid=(n_tiles // 2 + 1,),
        in_specs=[
            pl.BlockSpec(memory_space=pl.ANY),
            _const_spec((1, d)),
            _const_spec((d, d_in)),
            _const_spec(conv_w.shape),
            _const_spec(lru_conv_w.shape),
            _const_spec((1, W_LRU)),
            _const_spec(wgate.shape),
            _const_spec((1, W_LRU)),
            _const_spec((1, W_LRU)),
            _const_spec((1, W_LRU)),
            _const_spec(w_out.shape),
        ],
        out_specs=pl.BlockSpec(memory_space=pl.ANY),
        out_shape=jax.ShapeDtypeStruct((t // SUBLANES, SUBLANES, d), F32),
        scratch_shapes=[
            pltpu.VMEM((6, seg, SUBLANES, d), F32),
            pltpu.VMEM((tm, d_in), F32),
            pltpu.VMEM((tm, d_in), F32),
            pltpu.VMEM((2, seg, SUBLANES, d), F32),
            pltpu.VMEM(((conv_w.shape[0] - 1) * SUBLANES, W_CONV), F32),
            pltpu.VMEM(((lru_conv_w.shape[0] - 1) * SUBLANES, W_LRU), F32),
            pltpu.VMEM((SUBLANES, W_LRU), F32),
            pltpu.SemaphoreType.DMA((6,)),
            pltpu.SemaphoreType.DMA((2,)),
        ],
        compiler_params=pltpu.CompilerParams(
            dimension_semantics=("arbitrary",), vmem_limit_bytes=VMEM_LIMIT_BYTES),
        name="mixer",
    )(x_in, row(gn), w_in.astype(BF16), conv_w, lru_conv_w, row(lru_conv_b), wgate,
      row(ba), row(bx), row(lam), w_out.astype(BF16))
    return out.reshape(t, d)


def _swiglu_rows(h, wg_ref, wu_ref, wd_ref):
    d_ff = wg_ref.shape[-1]
    acc = jnp.zeros((h.shape[0], wd_ref.shape[-1]), F32)
    for c in range(d_ff // FF_CHUNK):
        cols = slice(c * FF_CHUNK, (c + 1) * FF_CHUNK)
        gate = jnp.dot(h, wg_ref[:, cols], preferred_element_type=F32)
        up = jnp.dot(h, wu_ref[:, cols], preferred_element_type=F32)
        act = (gate * _sigmoid(gate) * up).astype(BF16)
        acc = acc + jnp.dot(act, wd_ref[cols, :], preferred_element_type=F32)
    return acc


def _ffn_kernel(x_ref, gn_ref, wg_ref, wu_ref, wd_ref, o_ref):
    x = x_ref[...]
    h = _rms_norm(x, gn_ref[...]).astype(BF16)
    o_ref[...] = x + _swiglu_rows(h, wg_ref, wu_ref, wd_ref)


def _ffn_dense(x2, gn, wg, wu, wd):
    t, d = x2.shape
    tm = TM_FFN
    return pl.pallas_call(
        _ffn_kernel,
        grid=(t // tm,),
        in_specs=[
            pl.BlockSpec((tm, d), lambda i: (i, 0)),
            _const_spec((1, d)),
            _const_spec(wg.shape),
            _const_spec(wu.shape),
            _const_spec(wd.shape),
        ],
        out_specs=pl.BlockSpec((tm, d), lambda i: (i, 0)),
        out_shape=jax.ShapeDtypeStruct(x2.shape, F32),
        compiler_params=pltpu.CompilerParams(
            dimension_semantics=("arbitrary",), vmem_limit_bytes=VMEM_LIMIT_BYTES),
        name="ffn_dense",
    )(x2, gn.reshape(1, -1), wg.astype(BF16), wu.astype(BF16), wd.astype(BF16))


def _split_bf16(a):
    hi = a.astype(BF16)
    return hi, (a - hi.astype(F32)).astype(BF16)


def _router_kernel(x_ref, gn_ref, wr_hi_ref, wr_lo_ref, meta_ref, gates_ref, counts_ref, run):
    i = pl.program_id(0)
    tm = x_ref.shape[0]

    @pl.when(i == 0)
    def _():
        run[...] = jnp.zeros((1, LANES), F32)

    h_hi, h_lo = _split_bf16(_rms_norm(x_ref[...], gn_ref[...]))
    logits = (jnp.dot(h_hi, wr_hi_ref[...], preferred_element_type=F32)
              + jnp.dot(h_lo, wr_hi_ref[...], preferred_element_type=F32)
              + jnp.dot(h_hi, wr_lo_ref[...], preferred_element_type=F32))
    lane = lax.broadcasted_iota(I32, (tm, LANES), 1)
    neg_inf = jnp.float32(-jnp.inf)
    logits = jnp.where(lane < N_EXPERTS, logits, neg_inf)
    m1 = jnp.max(logits, axis=-1, keepdims=True)
    i1 = jnp.min(jnp.where(logits == m1, lane, LANES), axis=-1, keepdims=True)
    rest = jnp.where(lane == i1, neg_inf, logits)
    m2 = jnp.max(rest, axis=-1, keepdims=True)
    i2 = jnp.min(jnp.where(rest == m2, lane, LANES), axis=-1, keepdims=True)
    e = jnp.exp(m2 - m1)
    g1 = 1.0 / (1.0 + e)
    g2 = e * g1

    sel1 = lane == i1
    sel2 = lane == i2
    hot = jnp.where(sel1 | sel2, 1.0, 0.0)
    rows = lax.broadcasted_iota(I32, (tm, tm), 0)
    cols = lax.broadcasted_iota(I32, (tm, tm), 1)
    strict_lower = jnp.where(cols < rows, 1.0, 0.0).astype(BF16)
    before = jnp.dot(strict_lower, hot.astype(BF16), preferred_element_type=F32) + run[...]
    rank1 = jnp.sum(jnp.where(sel1, before, 0.0), axis=-1, keepdims=True)
    rank2 = jnp.sum(jnp.where(sel2, before, 0.0), axis=-1, keepdims=True)
    run[...] = run[...] + jnp.sum(hot, axis=0, keepdims=True)

    meta = jnp.where(lane == 0, i1, jnp.where(lane == 1, i2, jnp.where(
        lane == 2, rank1.astype(I32), jnp.where(lane == 3, rank2.astype(I32), 0))))
    meta_ref[...] = meta
    gates_ref[...] = jnp.where(lane == 0, g1, jnp.where(lane == 1, g2, 0.0))
    counts_ref[...] = jnp.broadcast_to(run[...], (SUBLANES, LANES))


def _router(x2, gn, w_router):
    t, d = x2.shape
    tm = TM_ROUTE
    wr_hi, wr_lo = _split_bf16(jnp.zeros((d, LANES), F32).at[:, :N_EXPERTS].set(w_router))
    return pl.pallas_call(
        _router_kernel,
        grid=(t // tm,),
        in_specs=[
            pl.BlockSpec((tm, d), lambda i: (i, 0)),
            _const_spec((1, d)),
            _const_spec((d, LANES)),
            _const_spec((d, LANES)),
        ],
        out_specs=[
            pl.BlockSpec((tm, LANES), lambda i: (i, 0)),
            pl.BlockSpec((tm, LANES), lambda i: (i, 0)),
            pl.BlockSpec((SUBLANES, LANES), lambda i: (0, 0)),
        ],
        out_shape=[
            jax.ShapeDtypeStruct((t, LANES), I32),
            jax.ShapeDtypeStruct((t, LANES), F32),
            jax.ShapeDtypeStruct((SUBLANES, LANES), F32),
        ],
        scratch_shapes=[pltpu.VMEM((1, LANES), F32)],
        compiler_params=pltpu.CompilerParams(
            dimension_semantics=("arbitrary",), vmem_limit_bytes=VMEM_LIMIT_BYTES),
        name="router",
    )(x2, gn.reshape(1, -1), wr_hi, wr_lo)


def _row_copy(src_ref, src_row, dst_ref, dst_row, sem):
    return pltpu.make_async_copy(src_ref.at[pl.ds(src_row, 1), :], dst_ref.at[pl.ds(dst_row, 1), :], sem)


def _dispatch_kernel(pad_ref, pos_ref, x_ref, xs_hbm, ztile, sem, zsem):
    i = pl.program_id(0)
    tile_rows = ztile.shape[0]

    @pl.when(i == 0)
    def _():
        ztile[...] = jnp.zeros(ztile.shape, F32)
        for e in range(N_EXPERTS):
            start = pad_ref[0, e]
            n = pad_ref[1, e]

            def fill(q, c):
                _row_copy(ztile, 0, xs_hbm, start + q, zsem).start()
                return c

            def drain(q, c):
                _row_copy(ztile, 0, xs_hbm, start, zsem).wait()
                return c

            lax.fori_loop(0, n, fill, 0)
            lax.fori_loop(0, n, drain, 0)

        def tail_copy(q):
            r0 = pl.multiple_of(q * tile_rows, tile_rows)
            return pltpu.make_async_copy(ztile, xs_hbm.at[pl.ds(r0, tile_rows), :], zsem)

        def fill_tail(q, c):
            tail_copy(q).start()
            return c

        def drain_tail(q, c):
            tail_copy(q).wait()
            return c

        n_valid = pad_ref[2, 0]
        n_tiles = xs_hbm.shape[0] // tile_rows
        lax.fori_loop(n_valid, n_tiles, fill_tail, 0)
        lax.fori_loop(n_valid, n_tiles, drain_tail, 0)

    def issue(q, c):
        for s in range(SUBLANES):
            for k in range(TOP_K):
                dst = pos_ref[0, 0, TOP_K * (q * SUBLANES + s) + k]
                pltpu.make_async_copy(x_ref.at[q, pl.ds(s, 1), :], xs_hbm.at[pl.ds(dst, 1), :],
                                      sem.at[k]).start(priority=k)
        return c

    n_groups = x_ref.shape[0]
    lax.fori_loop(0, n_groups, issue, 0)
    for k in range(TOP_K):
        for _ in range(n_groups * SUBLANES // tile_rows):
            pltpu.make_async_copy(ztile, xs_hbm.at[pl.ds(0, tile_rows), :], sem.at[k]).wait()


def _dispatch(x2, pos, pad_info, n_rows):
    t, d = x2.shape
    tm = TM_DISP
    pos3 = pos.reshape(t // tm, 1, TOP_K * tm)
    grid_spec = pltpu.PrefetchScalarGridSpec(
        num_scalar_prefetch=1,
        grid=(t // tm,),
        in_specs=[
            pl.BlockSpec((1, 1, TOP_K * tm), lambda i, pad: (i, 0, 0), memory_space=pltpu.SMEM),
            pl.BlockSpec((tm // SUBLANES, SUBLANES, d), lambda i, pad: (i, 0, 0)),
        ],
        out_specs=pl.BlockSpec(memory_space=pl.ANY),
        scratch_shapes=[
            pltpu.VMEM((TM_EXP, d), F32),
            pltpu.SemaphoreType.DMA((TOP_K,)),
            pltpu.SemaphoreType.DMA(()),
        ],
    )
    return pl.pallas_call(
        _dispatch_kernel,
        grid_spec=grid_spec,
        out_shape=jax.ShapeDtypeStruct((n_rows, d), F32),
        compiler_params=pltpu.CompilerParams(
            dimension_semantics=("arbitrary",), vmem_limit_bytes=VMEM_LIMIT_BYTES),
        name="dispatch",
    )(pad_info, pos3, x2.reshape(t // SUBLANES, SUBLANES, d))


def _expert_kernel(texp_ref, nvalid_ref, xs_ref, gn_ref, wg_ref, wu_ref, wd_ref, ys_ref):
    i = pl.program_id(0)

    @pl.when(i < nvalid_ref[0])
    def _():
        h = _rms_norm(xs_ref[...], gn_ref[...]).astype(BF16)
        ys_ref[...] = _swiglu_rows(h, wg_ref, wu_ref, wd_ref)

    @pl.when(i >= nvalid_ref[0])
    def _():
        ys_ref[...] = jnp.zeros(ys_ref.shape, F32)


def _experts(xs, gn, wg, wu, wd, tile_expert, n_valid):
    n_rows, d = xs.shape
    tm = TM_EXP
    d_ff = wg.shape[-1]
    last = lambda i, nv: jnp.maximum(jnp.minimum(i, nv[0] - 1), 0)
    tile = lambda i, te, nv: (last(i, nv), 0)
    expert = lambda i, te, nv: (te[last(i, nv)], 0, 0)
    grid_spec = pltpu.PrefetchScalarGridSpec(
        num_scalar_prefetch=2,
        grid=(n_rows // tm,),
        in_specs=[
            pl.BlockSpec((tm, d), tile),
            pl.BlockSpec((1, d), lambda i, te, nv: (0, 0)),
            pl.BlockSpec((None, d, d_ff), expert),
            pl.BlockSpec((None, d, d_ff), expert),
            pl.BlockSpec((None, d_ff, d), expert),
        ],
        out_specs=pl.BlockSpec((tm, d), lambda i, te, nv: (i, 0)),
    )
    return pl.pallas_call(
        _expert_kernel,
        grid_spec=grid_spec,
        out_shape=jax.ShapeDtypeStruct((n_rows, d), F32),
        compiler_params=pltpu.CompilerParams(
            dimension_semantics=("arbitrary",), vmem_limit_bytes=VMEM_LIMIT_BYTES),
        name="experts",
    )(tile_expert, n_valid, xs, gn.reshape(1, -1), wg.astype(BF16), wu.astype(BF16), wd.astype(BF16))


def _combine_kernel(pos_ref, x_ref, gates_ref, gf_ref, ys_hbm, o_ref, buf, sem):
    tm = x_ref.shape[0]

    def issue(r, c):
        for k in range(TOP_K):
            _row_copy(ys_hbm, pos_ref[0, 0, TOP_K * r + k], buf.at[k], r, sem.at[k]).start(priority=k)
        return c

    lax.fori_loop(0, tm, issue, 0, unroll=ROW_DMA_UNROLL)
    for k in range(TOP_K):
        pltpu.make_async_copy(ys_hbm.at[pl.ds(0, tm), :], buf.at[k], sem.at[k]).wait()
    gates = gates_ref[...]
    y = x_ref[...] + gates[:, 0:1] * buf[0] + gates[:, 1:2] * buf[1]
    o_ref[...] = _rms_norm(y, gf_ref[...])


def _combine(x2, gates, pos, ys, g_final):
    t, d = x2.shape
    tm = TM_DISP
    pos3 = pos.reshape(t // tm, 1, TOP_K * tm)
    return pl.pallas_call(
        _combine_kernel,
        grid=(t // tm,),
        in_specs=[
            pl.BlockSpec((1, 1, TOP_K * tm), lambda i: (i, 0, 0), memory_space=pltpu.SMEM),
            pl.BlockSpec((tm, d), lambda i: (i, 0)),
            pl.BlockSpec((tm, LANES), lambda i: (i, 0)),
            _const_spec((1, d)),
            pl.BlockSpec(memory_space=pl.ANY),
        ],
        out_specs=pl.BlockSpec((tm, d), lambda i: (i, 0)),
        out_shape=jax.ShapeDtypeStruct(x2.shape, F32),
        scratch_shapes=[pltpu.VMEM((TOP_K, tm, d), F32), pltpu.SemaphoreType.DMA((TOP_K,))],
        compiler_params=pltpu.CompilerParams(
            dimension_semantics=("arbitrary",), vmem_limit_bytes=VMEM_LIMIT_BYTES),
        name="combine",
    )(pos3, x2, gates, g_final.reshape(1, -1), ys)


def _moe_layer(x2, gn, w_router, wg, wu, wd, g_final):
    t, _ = x2.shape
    tm = TM_EXP
    n_tiles = (TOP_K * t) // tm + N_EXPERTS
    meta, gates, counts = _router(x2, gn, w_router)

    counts = counts[0, :N_EXPERTS].astype(I32)
    padded = ((counts + tm - 1) // tm) * tm
    ends = jnp.cumsum(padded)
    starts = ends - padded
    idx = meta[:, 0:TOP_K]
    rank = meta[:, TOP_K:2 * TOP_K]
    pos = (starts[idx] + rank).reshape(-1)
    tile_start = jnp.arange(n_tiles, dtype=I32) * tm
    tile_expert = jnp.minimum(
        jnp.sum((tile_start[:, None] >= ends[None, :]).astype(I32), axis=1), N_EXPERTS - 1).astype(I32)
    n_valid = (ends[-1:] // tm).astype(I32)
    pad_info = jnp.stack([starts + counts, padded - counts, jnp.broadcast_to(n_valid, (N_EXPERTS,))]).astype(I32)

    xs = _dispatch(x2, pos, pad_info, n_tiles * tm)
    ys = _experts(xs, gn, wg, wu, wd, tile_expert, n_valid)
    return _combine(x2, gates, pos, ys, g_final)


def kernel(x, norm_mix, norm_ffn, norm_final, w_in, conv_w, lru_conv_w, lru_conv_b, lru_wa, lru_ba, lru_wx,
           lru_bx, lru_lambda, w_out, ffn_w_gate, ffn_w_up, ffn_w_down, w_router, moe_w_gate, moe_w_up,
           moe_w_down):
    bsz, seq, d = x.shape
    depth = w_in.shape[0]
    assert depth == 2, "layer 0 uses the dense FFN, layer 1 the routed FFN followed by the final norm"
    x2 = x.reshape(bsz * seq, d)
    for l in range(depth):
        x2 = _mixer(x2, seq, l == 0, norm_mix[l], w_in[l], conv_w[l], lru_conv_w[l], lru_conv_b[l], lru_wa[l],
                    lru_ba[l], lru_wx[l], lru_bx[l], lru_lambda[l], w_out[l])
        if l % 2 == 0:
            x2 = _ffn_dense(x2, norm_ffn[l], ffn_w_gate[l // 2], ffn_w_up[l // 2], ffn_w_down[l // 2])
        else:
            x2 = _moe_layer(x2, norm_ffn[l], w_router[l // 2], moe_w_gate[l // 2], moe_w_up[l // 2],
                            moe_w_down[l // 2], norm_final)
    return _from_segment_major(x2, TM_MIX).reshape(bsz, seq, d)
```

```python
import functools

import jax
import jax.numpy as jnp
from jax import lax
from jax.experimental import pallas as pl
from jax.experimental.pallas import tpu as pltpu

F32 = jnp.float32
BF16 = jnp.bfloat16
I32 = jnp.int32

EPS = 1e-6
RG_C = 8.0
W_CONV = 512
W_LRU = 512
N_EXPERTS = 8
TOP_K = 2

SUBLANES = 8
LANES = 128
MXU_DIM = 256
VMEM_LIMIT_BYTES = 56 * 1024 * 1024

TM_MIX = 256
TM_FFN = 512
FF_CHUNK = 512
TM_ROUTE = 512
TM_DISP = 512
ROW_DMA_UNROLL = 8
TM_EXP = 512


def _const_spec(shape):
    nd = len(shape)
    return pl.BlockSpec(shape, lambda *_: (0,) * nd, pipeline_mode=pl.Buffered(1))


def _rms_norm(x, g):
    ms = jnp.mean(x * x, axis=-1, keepdims=True)
    return x * lax.rsqrt(ms + EPS) * g


def _sigmoid(x):
    return 1.0 / (1.0 + jnp.exp(-x))


def _gelu_tanh(x):
    c = 0.7978845608028654
    return 0.5 * x * (1.0 + jnp.tanh(c * (x + 0.044715 * (x * x * x))))


def _segment_major_fetch(hbm, tile, buf, slot, sem):
    seg = buf.shape[1]
    return [pltpu.make_async_copy(hbm.at[pl.ds(tile * (seg * SUBLANES) + i * seg, seg), :],
                                  buf.at[slot, :, i, :], sem.at[slot]) for i in range(SUBLANES)]


def _from_segment_major(a, tm):
    rows, d = a.shape
    seg = tm // SUBLANES
    return a.reshape(rows // tm, seg, SUBLANES, d).swapaxes(1, 2).reshape(rows, d)


def _segment_halo(cur, prev, n):
    first = lax.broadcasted_iota(I32, (SUBLANES, cur.shape[1]), 0) == 0
    out = []
    for k in range(n):
        rows = slice(k * SUBLANES, (k + 1) * SUBLANES)
        out.append(jnp.where(first, pltpu.roll(prev[rows, :], 1, axis=0), pltpu.roll(cur[rows, :], 1, axis=0)))
    return jnp.concatenate(out, axis=0)


def _mixer_kernel(x_hbm, gn_ref, win_ref, cw_ref, lcw_ref, lcb_ref, wgate_ref, ba_ref, bx_ref, lam_ref,
                  wout_ref, side_ref, o_hbm, side_out_ref, xbuf, p_even, p_odd, obuf, ztail, utail, hcarry,
                  xsem, osem, *,
                  n_tiles, tiles_per_seq, time_ordered_input):
    k = pl.program_id(0)
    last_step = pl.num_programs(0) - 1
    seg = xbuf.shape[1]
    tm = seg * SUBLANES
    d = xbuf.shape[3]
    n_x = xbuf.shape[0]
    t_even = 2 * k
    t_odd = 2 * k + 1

    def fetch(tile):
        slot = tile % n_x
        if time_ordered_input:
            return _segment_major_fetch(x_hbm, tile, xbuf, slot, xsem)
        return [pltpu.make_async_copy(x_hbm.at[pl.ds(tile * seg, seg)], xbuf.at[slot], xsem.at[slot])]

    n_o = obuf.shape[0]

    def out_slot(tile):
        return (tile + n_o) % n_o

    def write_back(tile):
        slot = out_slot(tile)
        return [pltpu.make_async_copy(obuf.at[slot], o_hbm.at[pl.ds(tile * seg, seg)], osem.at[slot])]

    @pl.when(k == 0)
    def _():
        for c in fetch(0) + fetch(1):
            c.start()
        xbuf[n_x - 1] = jnp.zeros(xbuf.shape[1:], F32)
        p_odd[...] = jnp.zeros(p_odd.shape, F32)
        ztail[...] = jnp.zeros(ztail.shape, F32)
        utail[...] = jnp.zeros(utail.shape, F32)
        hcarry[...] = jnp.zeros(hcarry.shape, F32)

    @pl.when(k < last_step)
    def _():
        for c in fetch(t_even) + fetch(t_odd):
            c.wait()

    @pl.when(k + 1 < last_step)
    def _():
        for c in fetch(t_even + 2) + fetch(t_odd + 2):
            c.start()

    @pl.when(k >= 3)
    def _():
        for c in write_back(t_even - 1 - n_o):
            c.wait()

    @pl.when(k >= 2)
    def _():
        for c in write_back(t_even - n_o):
            c.wait()

    def project(tile, p_ref):
        xa = xbuf[jnp.minimum(tile, n_tiles - 1) % n_x].reshape(tm, d)
        ha = _rms_norm(xa, gn_ref[...]).astype(BF16)
        p_ref[...] = jnp.dot(ha, win_ref[...], preferred_element_type=F32)

    def finish(tile, p_ref, state):
        z_tail, u_tail, h_last = state
        fresh = (tile + tiles_per_seq) % tiles_per_seq == 0
        z_tail = jnp.where(fresh, 0.0, z_tail)
        u_tail = jnp.where(fresh, 0.0, u_tail)
        h_last = jnp.where(fresh, 0.0, h_last)
        c_gate = p_ref[:, 0:W_CONV]
        b_gate = p_ref[:, W_CONV:2 * W_CONV]
        v = p_ref[:, 2 * W_CONV:3 * W_CONV]
        u = p_ref[:, 3 * W_CONV:3 * W_CONV + W_LRU]
        g = p_ref[:, 3 * W_CONV + W_LRU:]

        z = c_gate * v
        nz = z_tail.shape[0]
        z_new_tail = z[tm - nz:, :]
        zx = jnp.concatenate([_segment_halo(z_new_tail, z_tail, nz // SUBLANES), z], axis=0)
        cw = cw_ref[...]
        conv = cw[2:3, :] * z + cw[1:2, :] * zx[nz - SUBLANES:nz - SUBLANES + tm, :] + cw[0:1, :] * zx[0:tm, :]
        y_a = b_gate * conv

        nu = u_tail.shape[0]
        u_new_tail = u[tm - nu:, :]
        ux = jnp.concatenate([_segment_halo(u_new_tail, u_tail, nu // SUBLANES), u], axis=0)
        lw = lcw_ref[...]
        uc = (lw[3:4, :] * u + lw[2:3, :] * ux[2 * SUBLANES:2 * SUBLANES + tm, :]
              + lw[1:2, :] * ux[SUBLANES:SUBLANES + tm, :] + lw[0:1, :] * ux[0:tm, :] + lcb_ref[...])

        ucb = uc.astype(BF16)
        pre0 = jnp.dot(ucb[:, 0:MXU_DIM], wgate_ref[0], preferred_element_type=F32)
        pre1 = jnp.dot(ucb[:, MXU_DIM:], wgate_ref[1], preferred_element_type=F32)
        pre_a = jnp.concatenate([pre0[:, :MXU_DIM], pre1[:, :MXU_DIM]], axis=1) + ba_ref[...]
        pre_x = jnp.concatenate([pre0[:, MXU_DIM:], pre1[:, MXU_DIM:]], axis=1) + bx_ref[...]
        r = _sigmoid(pre_a)
        i_gate = _sigmoid(pre_x)
        neg_lam = -lam_ref[...]
        softplus = jnp.maximum(neg_lam, 0.0) + jnp.log1p(jnp.exp(-jnp.abs(neg_lam)))
        log_a = (-RG_C) * r * softplus
        a = jnp.exp(log_a)
        th = jnp.tanh(log_a)
        b = jnp.sqrt((-2.0 * th) / (1.0 - th)) * (i_gate * uc)

        a3 = a.reshape(seg, SUBLANES, W_LRU)
        b3 = b.reshape(seg, SUBLANES, W_LRU)
        prod = [a3[0]]
        part = [b3[0]]
        for j in range(1, seg):
            prod.append(a3[j] * prod[j - 1])
            part.append(a3[j] * part[j - 1] + b3[j])
        init_rows = []
        for i in range(SUBLANES):
            init_rows.append(h_last)
            h_last = prod[seg - 1][i:i + 1, :] * h_last + part[seg - 1][i:i + 1, :]
        init = jnp.concatenate(init_rows, axis=0)
        hs = jnp.stack([part[j] + prod[j] * init for j in range(seg)], axis=0).reshape(tm, W_LRU)

        y_b = hs * _gelu_tanh(g)
        y = jnp.concatenate([y_a, y_b], axis=1).astype(BF16)
        xc = xbuf[(tile + n_x) % n_x].reshape(tm, d)
        out = xc + jnp.dot(y, wout_ref[...], preferred_element_type=F32)
        return out.reshape(seg, SUBLANES, d), (z_new_tail, u_new_tail, h_last)

    side_out_ref[...] = side_ref[...].astype(BF16)
    state = (ztail[...], utail[...], hcarry[0:1, :])
    project(t_even, p_even)
    obuf[out_slot(t_even - 1)], state = finish(t_even - 1, p_odd, state)
    project(t_odd, p_odd)
    obuf[out_slot(t_even)], state = finish(t_even, p_even, state)
    ztail[...], utail[...] = state[0], state[1]
    hcarry[...] = jnp.broadcast_to(state[2], hcarry.shape)

    @pl.when(k >= 1)
    def _():
        for c in write_back(t_even - 1):
            c.start()

    @pl.when(k < last_step)
    def _():
        for c in write_back(t_even):
            c.start()

    @pl.when(k == last_step)
    def _():
        for c in write_back(t_even - 3) + write_back(t_even - 2) + write_back(t_even - 1):
            c.wait()


def _block_diag_halves(w):
    nh, dh, _ = w.shape
    per = MXU_DIM // dh
    w4 = w.reshape(nh // per, per, dh, dh)
    eye = jnp.eye(per, dtype=w.dtype)
    return jnp.einsum("kaij,ab->kaibj", w4, eye).reshape(nh // per, MXU_DIM, MXU_DIM)


def _side_cast_specs(side, n_blocks):
    rows, cols = side.shape
    block = (rows // n_blocks, cols)
    index = lambda i: (jnp.minimum(i, n_blocks - 1), 0)
    return pl.BlockSpec(block, index), pl.BlockSpec(block, index), jax.ShapeDtypeStruct(side.shape, BF16)


def _mixer(x2, seq, time_ordered_input, gn, w_in, conv_w, lru_conv_w, lru_conv_b, wa, ba, wx, bx, lam, w_out,
           side):
    t, d = x2.shape
    d_in = w_in.shape[1]
    tm = TM_MIX
    seg = tm // SUBLANES
    n_tiles = t // tm
    assert n_tiles % 2 == 0 and (seq // tm) % 2 == 0
    wgate = jnp.concatenate([_block_diag_halves(wa), _block_diag_halves(wx)], axis=-1).astype(BF16)
    row = lambda a: a.reshape(1, -1)
    x_in = x2 if time_ordered_input else x2.reshape(t // SUBLANES, SUBLANES, d)
    side_in, side_out, side_shape = _side_cast_specs(side, n_tiles // 2)
    out, side_bf16 = pl.pallas_call(
        functools.partial(_mixer_kernel, n_tiles=n_tiles, tiles_per_seq=seq // tm,
                          time_ordered_input=time_ordered_input),
        grid=(n_tiles // 2 + 1,),
        in_specs=[
            pl.BlockSpec(memory_space=pl.ANY),
            _const_spec((1, d)),
            _const_spec((d, d_in)),
            _const_spec(conv_w.shape),
            _const_spec(lru_conv_w.shape),
            _const_spec((1, W_LRU)),
            _const_spec(wgate.shape),
            _const_spec((1, W_LRU)),
            _const_spec((1, W_LRU)),
            _const_spec((1, W_LRU)),
            _const_spec(w_out.shape),
            side_in,
        ],
        out_specs=[pl.BlockSpec(memory_space=pl.ANY), side_out],
        out_shape=[jax.ShapeDtypeStruct((t // SUBLANES, SUBLANES, d), F32), side_shape],
        scratch_shapes=[
            pltpu.VMEM((6, seg, SUBLANES, d), F32),
            pltpu.VMEM((tm, d_in), F32),
            pltpu.VMEM((tm, d_in), F32),
            pltpu.VMEM((4, seg, SUBLANES, d), F32),
            pltpu.VMEM(((conv_w.shape[0] - 1) * SUBLANES, W_CONV), F32),
            pltpu.VMEM(((lru_conv_w.shape[0] - 1) * SUBLANES, W_LRU), F32),
            pltpu.VMEM((SUBLANES, W_LRU), F32),
            pltpu.SemaphoreType.DMA((6,)),
            pltpu.SemaphoreType.DMA((4,)),
        ],
        compiler_params=pltpu.CompilerParams(
            dimension_semantics=("arbitrary",), vmem_limit_bytes=VMEM_LIMIT_BYTES),
        name="mixer",
    )(x_in, row(gn), w_in.astype(BF16), conv_w, lru_conv_w, row(lru_conv_b), wgate,
      row(ba), row(bx), row(lam), w_out.astype(BF16), side)
    return out.reshape(t, d), side_bf16


def _swiglu_rows(h, wg_ref, wu_ref, wd_ref):
    d_ff = wg_ref.shape[-1]
    acc = jnp.zeros((h.shape[0], wd_ref.shape[-1]), F32)
    for c in range(d_ff // FF_CHUNK):
        cols = slice(c * FF_CHUNK, (c + 1) * FF_CHUNK)
        gate = jnp.dot(h, wg_ref[:, cols], preferred_element_type=F32)
        up = jnp.dot(h, wu_ref[:, cols], preferred_element_type=F32)
        act = (gate * _sigmoid(gate) * up).astype(BF16)
        acc = acc + jnp.dot(act, wd_ref[cols, :], preferred_element_type=F32)
    return acc


def _ffn_kernel(x_ref, gn_ref, wg_ref, wu_ref, wd_ref, side_ref, o_ref, side_out_ref):
    side_out_ref[...] = side_ref[...].astype(BF16)
    x = x_ref[...]
    h = _rms_norm(x, gn_ref[...]).astype(BF16)
    o_ref[...] = x + _swiglu_rows(h, wg_ref, wu_ref, wd_ref)


def _ffn_dense(x2, gn, wg, wu, wd, side):
    t, d = x2.shape
    tm = TM_FFN
    side_in, side_out, side_shape = _side_cast_specs(side, t // tm)
    return pl.pallas_call(
        _ffn_kernel,
        grid=(t // tm,),
        in_specs=[
            pl.BlockSpec((tm, d), lambda i: (i, 0)),
            _const_spec((1, d)),
            _const_spec(wg.shape),
            _const_spec(wu.shape),
            _const_spec(wd.shape),
            side_in,
        ],
        out_specs=[pl.BlockSpec((tm, d), lambda i: (i, 0)), side_out],
        out_shape=[jax.ShapeDtypeStruct(x2.shape, F32), side_shape],
        compiler_params=pltpu.CompilerParams(
            dimension_semantics=("arbitrary",), vmem_limit_bytes=VMEM_LIMIT_BYTES),
        name="ffn_dense",
    )(x2, gn.reshape(1, -1), wg.astype(BF16), wu.astype(BF16), wd.astype(BF16), side)


def _split_bf16(a):
    hi = a.astype(BF16)
    return hi, (a - hi.astype(F32)).astype(BF16)


def _router_kernel(x_ref, gn_ref, wr_hi_ref, wr_lo_ref, meta_ref, gates_ref, counts_ref, run):
    i = pl.program_id(0)
    tm = x_ref.shape[0]

    @pl.when(i == 0)
    def _():
        run[...] = jnp.zeros((1, LANES), F32)

    h_hi, h_lo = _split_bf16(_rms_norm(x_ref[...], gn_ref[...]))
    logits = (jnp.dot(h_hi, wr_hi_ref[...], preferred_element_type=F32)
              + jnp.dot(h_lo, wr_hi_ref[...], preferred_element_type=F32)
              + jnp.dot(h_hi, wr_lo_ref[...], preferred_element_type=F32))
    lane = lax.broadcasted_iota(I32, (tm, LANES), 1)
    neg_inf = jnp.float32(-jnp.inf)
    logits = jnp.where(lane < N_EXPERTS, logits, neg_inf)
    m1 = jnp.max(logits, axis=-1, keepdims=True)
    i1 = jnp.min(jnp.where(logits == m1, lane, LANES), axis=-1, keepdims=True)
    rest = jnp.where(lane == i1, neg_inf, logits)
    m2 = jnp.max(rest, axis=-1, keepdims=True)
    i2 = jnp.min(jnp.where(rest == m2, lane, LANES), axis=-1, keepdims=True)
    e = jnp.exp(m2 - m1)
    g1 = 1.0 / (1.0 + e)
    g2 = e * g1

    sel1 = lane == i1
    sel2 = lane == i2
    hot = jnp.where(sel1 | sel2, 1.0, 0.0)
    rows = lax.broadcasted_iota(I32, (tm, tm), 0)
    cols = lax.broadcasted_iota(I32, (tm, tm), 1)
    strict_lower = jnp.where(cols < rows, 1.0, 0.0).astype(BF16)
    before = jnp.dot(strict_lower, hot.astype(BF16), preferred_element_type=F32) + run[...]
    rank1 = jnp.sum(jnp.where(sel1, before, 0.0), axis=-1, keepdims=True)
    rank2 = jnp.sum(jnp.where(sel2, before, 0.0), axis=-1, keepdims=True)
    run[...] = run[...] + jnp.sum(hot, axis=0, keepdims=True)

    meta = jnp.where(lane == 0, i1, jnp.where(lane == 1, i2, jnp.where(
        lane == 2, rank1.astype(I32), jnp.where(lane == 3, rank2.astype(I32), 0))))
    meta_ref[...] = meta
    gates_ref[...] = jnp.where(lane == 0, g1, jnp.where(lane == 1, g2, 0.0))
    counts_ref[...] = jnp.broadcast_to(run[...], (SUBLANES, LANES))


def _router(x2, gn, w_router):
    t, d = x2.shape
    tm = TM_ROUTE
    wr_hi, wr_lo = _split_bf16(jnp.zeros((d, LANES), F32).at[:, :N_EXPERTS].set(w_router))
    return pl.pallas_call(
        _router_kernel,
        grid=(t // tm,),
        in_specs=[
            pl.BlockSpec((tm, d), lambda i: (i, 0)),
            _const_spec((1, d)),
            _const_spec((d, LANES)),
            _const_spec((d, LANES)),
        ],
        out_specs=[
            pl.BlockSpec((tm, LANES), lambda i: (i, 0)),
            pl.BlockSpec((tm, LANES), lambda i: (i, 0)),
            pl.BlockSpec((SUBLANES, LANES), lambda i: (0, 0)),
        ],
        out_shape=[
            jax.ShapeDtypeStruct((t, LANES), I32),
            jax.ShapeDtypeStruct((t, LANES), F32),
            jax.ShapeDtypeStruct((SUBLANES, LANES), F32),
        ],
        scratch_shapes=[pltpu.VMEM((1, LANES), F32)],
        compiler_params=pltpu.CompilerParams(
            dimension_semantics=("arbitrary",), vmem_limit_bytes=VMEM_LIMIT_BYTES),
        name="router",
    )(x2, gn.reshape(1, -1), wr_hi, wr_lo)


def _row_copy(src_ref, src_row, dst_ref, dst_row, sem):
    return pltpu.make_async_copy(src_ref.at[pl.ds(src_row, 1), :], dst_ref.at[pl.ds(dst_row, 1), :], sem)


def _dispatch_kernel(pad_ref, pos_ref, x_ref, xs_hbm, ztile, sem, zsem):
    i = pl.program_id(0)
    tile_rows = ztile.shape[0]

    @pl.when(i == 0)
    def _():
        ztile[...] = jnp.zeros(ztile.shape, F32)
        for e in range(N_EXPERTS):
            start = pad_ref[0, e]
            n = pad_ref[1, e]

            def fill(q, c):
                _row_copy(ztile, 0, xs_hbm, start + q, zsem).start()
                return c

            def drain(q, c):
                _row_copy(ztile, 0, xs_hbm, start, zsem).wait()
                return c

            lax.fori_loop(0, n, fill, 0)
            lax.fori_loop(0, n, drain, 0)

        def tail_copy(q):
            r0 = pl.multiple_of(q * tile_rows, tile_rows)
            return pltpu.make_async_copy(ztile, xs_hbm.at[pl.ds(r0, tile_rows), :], zsem)

        def fill_tail(q, c):
            tail_copy(q).start()
            return c

        def drain_tail(q, c):
            tail_copy(q).wait()
            return c

        n_valid = pad_ref[2, 0]
        n_tiles = xs_hbm.shape[0] // tile_rows
        lax.fori_loop(n_valid, n_tiles, fill_tail, 0)
        lax.fori_loop(n_valid, n_tiles, drain_tail, 0)

    def issue(q, c):
        for s in range(SUBLANES):
            for k in range(TOP_K):
                dst = pos_ref[0, 0, TOP_K * (q * SUBLANES + s) + k]
                pltpu.make_async_copy(x_ref.at[q, pl.ds(s, 1), :], xs_hbm.at[pl.ds(dst, 1), :],
                                      sem.at[k]).start(priority=k)
        return c

    n_groups = x_ref.shape[0]
    lax.fori_loop(0, n_groups, issue, 0)
    for k in range(TOP_K):
        for _ in range(n_groups * SUBLANES // tile_rows):
            pltpu.make_async_copy(ztile, xs_hbm.at[pl.ds(0, tile_rows), :], sem.at[k]).wait()


def _dispatch(x2, pos, pad_info, n_rows):
    t, d = x2.shape
    tm = TM_DISP
    pos3 = pos.reshape(t // tm, 1, TOP_K * tm)
    grid_spec = pltpu.PrefetchScalarGridSpec(
        num_scalar_prefetch=1,
        grid=(t // tm,),
        in_specs=[
            pl.BlockSpec((1, 1, TOP_K * tm), lambda i, pad: (i, 0, 0), memory_space=pltpu.SMEM),
            pl.BlockSpec((tm // SUBLANES, SUBLANES, d), lambda i, pad: (i, 0, 0)),
        ],
        out_specs=pl.BlockSpec(memory_space=pl.ANY),
        scratch_shapes=[
            pltpu.VMEM((TM_EXP, d), F32),
            pltpu.SemaphoreType.DMA((TOP_K,)),
            pltpu.SemaphoreType.DMA(()),
        ],
    )
    return pl.pallas_call(
        _dispatch_kernel,
        grid_spec=grid_spec,
        out_shape=jax.ShapeDtypeStruct((n_rows, d), F32),
        compiler_params=pltpu.CompilerParams(
            dimension_semantics=("arbitrary",), vmem_limit_bytes=VMEM_LIMIT_BYTES),
        name="dispatch",
    )(pad_info, pos3, x2.reshape(t // SUBLANES, SUBLANES, d))


def _expert_kernel(texp_ref, nvalid_ref, xs_ref, gn_ref, wg_ref, wu_ref, wd_ref, ys_ref):
    i = pl.program_id(0)

    @pl.when(i < nvalid_ref[0])
    def _():
        h = _rms_norm(xs_ref[...], gn_ref[...]).astype(BF16)
        ys_ref[...] = _swiglu_rows(h, wg_ref, wu_ref, wd_ref)

    @pl.when(i >= nvalid_ref[0])
    def _():
        ys_ref[...] = jnp.zeros(ys_ref.shape, F32)


def _experts(xs, gn, wg, wu, wd, tile_expert, n_valid):
    n_rows, d = xs.shape
    tm = TM_EXP
    d_ff = wg.shape[-1]
    last = lambda i, nv: jnp.maximum(jnp.minimum(i, nv[0] - 1), 0)
    tile = lambda i, te, nv: (last(i, nv), 0)
    expert = lambda i, te, nv: (te[last(i, nv)], 0, 0)
    grid_spec = pltpu.PrefetchScalarGridSpec(
        num_scalar_prefetch=2,
        grid=(n_rows // tm,),
        in_specs=[
            pl.BlockSpec((tm, d), tile),
            pl.BlockSpec((1, d), lambda i, te, nv: (0, 0)),
            pl.BlockSpec((None, d, d_ff), expert),
            pl.BlockSpec((None, d, d_ff), expert),
            pl.BlockSpec((None, d_ff, d), expert),
        ],
        out_specs=pl.BlockSpec((tm, d), lambda i, te, nv: (i, 0)),
    )
    return pl.pallas_call(
        _expert_kernel,
        grid_spec=grid_spec,
        out_shape=jax.ShapeDtypeStruct((n_rows, d), F32),
        compiler_params=pltpu.CompilerParams(
            dimension_semantics=("arbitrary",), vmem_limit_bytes=VMEM_LIMIT_BYTES),
        name="experts",
    )(tile_expert, n_valid, xs, gn.reshape(1, -1), wg.astype(BF16), wu.astype(BF16), wd.astype(BF16))


def _combine_kernel(pos_ref, x_ref, gates_ref, gf_ref, ys_hbm, o_ref, buf, sem):
    tm = x_ref.shape[0]

    def issue(r, c):
        for k in range(TOP_K):
            _row_copy(ys_hbm, pos_ref[0, 0, TOP_K * r + k], buf.at[k], r, sem.at[k]).start(priority=k)
        return c

    lax.fori_loop(0, tm, issue, 0, unroll=ROW_DMA_UNROLL)
    for k in range(TOP_K):
        pltpu.make_async_copy(ys_hbm.at[pl.ds(0, tm), :], buf.at[k], sem.at[k]).wait()
    gates = gates_ref[...]
    y = x_ref[...] + gates[:, 0:1] * buf[0] + gates[:, 1:2] * buf[1]
    o_ref[...] = _rms_norm(y, gf_ref[...])


def _combine(x2, gates, pos, ys, g_final):
    t, d = x2.shape
    tm = TM_DISP
    pos3 = pos.reshape(t // tm, 1, TOP_K * tm)
    return pl.pallas_call(
        _combine_kernel,
        grid=(t // tm,),
        in_specs=[
            pl.BlockSpec((1, 1, TOP_K * tm), lambda i: (i, 0, 0), memory_space=pltpu.SMEM),
            pl.BlockSpec((tm, d), lambda i: (i, 0)),
            pl.BlockSpec((tm, LANES), lambda i: (i, 0)),
            _const_spec((1, d)),
            pl.BlockSpec(memory_space=pl.ANY),
        ],
        out_specs=pl.BlockSpec((tm, d), lambda i: (i, 0)),
        out_shape=jax.ShapeDtypeStruct(x2.shape, F32),
        scratch_shapes=[pltpu.VMEM((TOP_K, tm, d), F32), pltpu.SemaphoreType.DMA((TOP_K,))],
        compiler_params=pltpu.CompilerParams(
            dimension_semantics=("arbitrary",), vmem_limit_bytes=VMEM_LIMIT_BYTES),
        name="combine",
    )(pos3, x2, gates, g_final.reshape(1, -1), ys)


def _moe_layer(x2, gn, w_router, wg, wu, wd, g_final):
    t, _ = x2.shape
    tm = TM_EXP
    n_tiles = (TOP_K * t) // tm + N_EXPERTS
    meta, gates, counts = _router(x2, gn, w_router)

    counts = counts[0, :N_EXPERTS].astype(I32)
    padded = ((counts + tm - 1) // tm) * tm
    ends = jnp.cumsum(padded)
    starts = ends - padded
    idx = meta[:, 0:TOP_K]
    rank = meta[:, TOP_K:2 * TOP_K]
    pos = (starts[idx] + rank).reshape(-1)
    tile_start = jnp.arange(n_tiles, dtype=I32) * tm
    tile_expert = jnp.minimum(
        jnp.sum((tile_start[:, None] >= ends[None, :]).astype(I32), axis=1), N_EXPERTS - 1).astype(I32)
    n_valid = (ends[-1:] // tm).astype(I32)
    pad_info = jnp.stack([starts + counts, padded - counts, jnp.broadcast_to(n_valid, (N_EXPERTS,))]).astype(I32)

    xs = _dispatch(x2, pos, pad_info, n_tiles * tm)
    ys = _experts(xs, gn, wg, wu, wd, tile_expert, n_valid)
    return _combine(x2, gates, pos, ys, g_final)


def kernel(x, norm_mix, norm_ffn, norm_final, w_in, conv_w, lru_conv_w, lru_conv_b, lru_wa, lru_ba, lru_wx,
           lru_bx, lru_lambda, w_out, ffn_w_gate, ffn_w_up, ffn_w_down, w_router, moe_w_gate, moe_w_up,
           moe_w_down):
    bsz, seq, d = x.shape
    depth = w_in.shape[0]
    assert depth == 2, "layer 0 uses the dense FFN, layer 1 the routed FFN followed by the final norm"
    x2 = x.reshape(bsz * seq, d)
    mixer_args = lambda l: (norm_mix[l], w_in[l], conv_w[l], lru_conv_w[l], lru_conv_b[l], lru_wa[l], lru_ba[l],
                            lru_wx[l], lru_bx[l], lru_lambda[l], w_out[l])
    flat = lambda w: w.reshape(-1, w.shape[-1])
    x2, moe_gate = _mixer(x2, seq, True, *mixer_args(0), flat(moe_w_gate[0]))
    x2, moe_down = _ffn_dense(x2, norm_ffn[0], ffn_w_gate[0], ffn_w_up[0], ffn_w_down[0], flat(moe_w_down[0]))
    x2, moe_up = _mixer(x2, seq, False, *mixer_args(1), flat(moe_w_up[0]))
    x2 = _moe_layer(x2, norm_ffn[1], w_router[0], moe_gate.reshape(moe_w_gate[0].shape),
                    moe_up.reshape(moe_w_up[0].shape), moe_down.reshape(moe_w_down[0].shape), norm_final)
    return _from_segment_major(x2, TM_MIX).reshape(bsz, seq, d)
```

```python
import functools

import jax
import jax.numpy as jnp
from jax import lax
from jax.experimental import pallas as pl
from jax.experimental.pallas import tpu as pltpu

F32 = jnp.float32
BF16 = jnp.bfloat16
I32 = jnp.int32

EPS = 1e-6
RG_C = 8.0
W_CONV = 512
W_LRU = 512
N_EXPERTS = 8
TOP_K = 2

SUBLANES = 8
LANES = 128
MXU_DIM = 256
VMEM_LIMIT_BYTES = 56 * 1024 * 1024

TM_MIX = 256
TM_FFN = 512
FF_CHUNK = 512
TM_ROUTE = 512
TM_DISP = 512
TM_EXP = 512


def _const_spec(shape):
    nd = len(shape)
    return pl.BlockSpec(shape, lambda *_: (0,) * nd, pipeline_mode=pl.Buffered(1))


def _rms_norm(x, g):
    ms = jnp.mean(x * x, axis=-1, keepdims=True)
    return x * lax.rsqrt(ms + EPS) * g


def _sigmoid(x):
    return 1.0 / (1.0 + jnp.exp(-x))


def _gelu_tanh(x):
    c = 0.7978845608028654
    return 0.5 * x * (1.0 + jnp.tanh(c * (x + 0.044715 * (x * x * x))))


def _segment_major_fetch(hbm, tile, buf, slot, sem):
    seg = buf.shape[1]
    return [pltpu.make_async_copy(hbm.at[pl.ds(tile * (seg * SUBLANES) + i * seg, seg), :],
                                  buf.at[slot, :, i, :], sem.at[slot]) for i in range(SUBLANES)]


def _from_segment_major(a, tm):
    rows, d = a.shape
    seg = tm // SUBLANES
    return a.reshape(rows // tm, seg, SUBLANES, d).swapaxes(1, 2).reshape(rows, d)


def _segment_halo(cur, prev, n):
    first = lax.broadcasted_iota(I32, (SUBLANES, cur.shape[1]), 0) == 0
    out = []
    for k in range(n):
        rows = slice(k * SUBLANES, (k + 1) * SUBLANES)
        out.append(jnp.where(first, pltpu.roll(prev[rows, :], 1, axis=0), pltpu.roll(cur[rows, :], 1, axis=0)))
    return jnp.concatenate(out, axis=0)


def _mixer_kernel(*refs, n_side, n_tiles, tiles_per_seq, time_ordered_input):
    (x_hbm, gn_ref, win_ref, cw_ref, lcw_ref, lcb_ref, wgate_ref, ba_ref, bx_ref, lam_ref, wout_ref), refs = (
        refs[:11], refs[11:])
    side_refs, o_hbm, side_out_refs = refs[:n_side], refs[n_side], refs[n_side + 1:2 * n_side + 1]
    xbuf, p_even, p_odd, obuf, ztail, utail, hcarry, xsem, osem = refs[2 * n_side + 1:]
    k = pl.program_id(0)
    last_step = pl.num_programs(0) - 1
    seg = xbuf.shape[1]
    tm = seg * SUBLANES
    d = xbuf.shape[3]
    n_x = xbuf.shape[0]
    t_even = 2 * k
    t_odd = 2 * k + 1

    def fetch(tile):
        slot = tile % n_x
        if time_ordered_input:
            return _segment_major_fetch(x_hbm, tile, xbuf, slot, xsem)
        return [pltpu.make_async_copy(x_hbm.at[pl.ds(tile * seg, seg)], xbuf.at[slot], xsem.at[slot])]

    n_o = obuf.shape[0]

    def out_slot(tile):
        return (tile + n_o) % n_o

    def write_back(tile):
        slot = out_slot(tile)
        return [pltpu.make_async_copy(obuf.at[slot], o_hbm.at[pl.ds(tile * seg, seg)], osem.at[slot])]

    @pl.when(k == 0)
    def _():
        for c in fetch(0) + fetch(1):
            c.start()
        xbuf[n_x - 1] = jnp.zeros(xbuf.shape[1:], F32)
        p_odd[...] = jnp.zeros(p_odd.shape, F32)
        ztail[...] = jnp.zeros(ztail.shape, F32)
        utail[...] = jnp.zeros(utail.shape, F32)
        hcarry[...] = jnp.zeros(hcarry.shape, F32)

    @pl.when(k < last_step)
    def _():
        for c in fetch(t_even) + fetch(t_odd):
            c.wait()

    @pl.when(k + 1 < last_step)
    def _():
        for c in fetch(t_even + 2) + fetch(t_odd + 2):
            c.start()

    @pl.when(k >= 3)
    def _():
        for c in write_back(t_even - 1 - n_o):
            c.wait()

    @pl.when(k >= 2)
    def _():
        for c in write_back(t_even - n_o):
            c.wait()

    def project(tile, p_ref):
        xa = xbuf[jnp.minimum(tile, n_tiles - 1) % n_x].reshape(tm, d)
        ha = _rms_norm(xa, gn_ref[...]).astype(BF16)
        p_ref[...] = jnp.dot(ha, win_ref[...], preferred_element_type=F32)

    def finish(tile, p_ref, state):
        z_tail, u_tail, h_last = state
        fresh = (tile + tiles_per_seq) % tiles_per_seq == 0
        z_tail = jnp.where(fresh, 0.0, z_tail)
        u_tail = jnp.where(fresh, 0.0, u_tail)
        h_last = jnp.where(fresh, 0.0, h_last)
        c_gate = p_ref[:, 0:W_CONV]
        b_gate = p_ref[:, W_CONV:2 * W_CONV]
        v = p_ref[:, 2 * W_CONV:3 * W_CONV]
        u = p_ref[:, 3 * W_CONV:3 * W_CONV + W_LRU]
        g = p_ref[:, 3 * W_CONV + W_LRU:]

        z = c_gate * v
        nz = z_tail.shape[0]
        z_new_tail = z[tm - nz:, :]
        zx = jnp.concatenate([_segment_halo(z_new_tail, z_tail, nz // SUBLANES), z], axis=0)
        cw = cw_ref[...]
        conv = cw[2:3, :] * z + cw[1:2, :] * zx[nz - SUBLANES:nz - SUBLANES + tm, :] + cw[0:1, :] * zx[0:tm, :]
        y_a = b_gate * conv

        nu = u_tail.shape[0]
        u_new_tail = u[tm - nu:, :]
        ux = jnp.concatenate([_segment_halo(u_new_tail, u_tail, nu // SUBLANES), u], axis=0)
        lw = lcw_ref[...]
        uc = (lw[3:4, :] * u + lw[2:3, :] * ux[2 * SUBLANES:2 * SUBLANES + tm, :]
              + lw[1:2, :] * ux[SUBLANES:SUBLANES + tm, :] + lw[0:1, :] * ux[0:tm, :] + lcb_ref[...])

        ucb = uc.astype(BF16)
        pre0 = jnp.dot(ucb[:, 0:MXU_DIM], wgate_ref[0], preferred_element_type=F32)
        pre1 = jnp.dot(ucb[:, MXU_DIM:], wgate_ref[1], preferred_element_type=F32)
        pre_a = jnp.concatenate([pre0[:, :MXU_DIM], pre1[:, :MXU_DIM]], axis=1) + ba_ref[...]
        pre_x = jnp.concatenate([pre0[:, MXU_DIM:], pre1[:, MXU_DIM:]], axis=1) + bx_ref[...]
        r = _sigmoid(pre_a)
        i_gate = _sigmoid(pre_x)
        neg_lam = -lam_ref[...]
        softplus = jnp.maximum(neg_lam, 0.0) + jnp.log1p(jnp.exp(-jnp.abs(neg_lam)))
        log_a = (-RG_C) * r * softplus
        a = jnp.exp(log_a)
        th = jnp.tanh(log_a)
        b = jnp.sqrt((-2.0 * th) / (1.0 - th)) * (i_gate * uc)

        a3 = a.reshape(seg, SUBLANES, W_LRU)
        b3 = b.reshape(seg, SUBLANES, W_LRU)
        prod = [a3[0]]
        part = [b3[0]]
        for j in range(1, seg):
            prod.append(a3[j] * prod[j - 1])
            part.append(a3[j] * part[j - 1] + b3[j])
        init_rows = []
        for i in range(SUBLANES):
            init_rows.append(h_last)
            h_last = prod[seg - 1][i:i + 1, :] * h_last + part[seg - 1][i:i + 1, :]
        init = jnp.concatenate(init_rows, axis=0)
        hs = jnp.stack([part[j] + prod[j] * init for j in range(seg)], axis=0).reshape(tm, W_LRU)

        y_b = hs * _gelu_tanh(g)
        y = jnp.concatenate([y_a, y_b], axis=1).astype(BF16)
        xc = xbuf[(tile + n_x) % n_x].reshape(tm, d)
        out = xc + jnp.dot(y, wout_ref[...], preferred_element_type=F32)
        return out.reshape(seg, SUBLANES, d), (z_new_tail, u_new_tail, h_last)

    for side_ref, side_out_ref in zip(side_refs, side_out_refs):
        side_out_ref[...] = side_ref[...].astype(BF16)
    state = (ztail[...], utail[...], hcarry[0:1, :])
    project(t_even, p_even)
    obuf[out_slot(t_even - 1)], state = finish(t_even - 1, p_odd, state)
    project(t_odd, p_odd)
    obuf[out_slot(t_even)], state = finish(t_even, p_even, state)
    ztail[...], utail[...] = state[0], state[1]
    hcarry[...] = jnp.broadcast_to(state[2], hcarry.shape)

    @pl.when(k >= 1)
    def _():
        for c in write_back(t_even - 1):
            c.start()

    @pl.when(k < last_step)
    def _():
        for c in write_back(t_even):
            c.start()

    @pl.when(k == last_step)
    def _():
        for c in write_back(t_even - 3) + write_back(t_even - 2) + write_back(t_even - 1):
            c.wait()


def _block_diag_halves(w):
    nh, dh, _ = w.shape
    per = MXU_DIM // dh
    w4 = w.reshape(nh // per, per, dh, dh)
    eye = jnp.eye(per, dtype=w.dtype)
    return jnp.einsum("kaij,ab->kaibj", w4, eye).reshape(nh // per, MXU_DIM, MXU_DIM)


def _side_cast_specs(sides, n_blocks):
    index = lambda i: (jnp.minimum(i, n_blocks - 1), 0)
    specs = [pl.BlockSpec((s.shape[0] // n_blocks, s.shape[1]), index) for s in sides]
    return specs, list(specs), [jax.ShapeDtypeStruct(s.shape, BF16) for s in sides]


def _mixer(x2, seq, time_ordered_input, gn, w_in, conv_w, lru_conv_w, lru_conv_b, wa, ba, wx, bx, lam, w_out,
           sides):
    t, d = x2.shape
    d_in = w_in.shape[1]
    tm = TM_MIX
    seg = tm // SUBLANES
    n_tiles = t // tm
    assert n_tiles % 2 == 0 and (seq // tm) % 2 == 0
    wgate = jnp.concatenate([_block_diag_halves(wa), _block_diag_halves(wx)], axis=-1).astype(BF16)
    row = lambda a: a.reshape(1, -1)
    x_in = x2 if time_ordered_input else x2.reshape(t // SUBLANES, SUBLANES, d)
    side_in, side_out, side_shapes = _side_cast_specs(sides, n_tiles // 2)
    out, *sides_bf16 = pl.pallas_call(
        functools.partial(_mixer_kernel, n_side=len(sides), n_tiles=n_tiles, tiles_per_seq=seq // tm,
                          time_ordered_input=time_ordered_input),
        grid=(n_tiles // 2 + 1,),
        in_specs=[
            pl.BlockSpec(memory_space=pl.ANY),
            _const_spec((1, d)),
            _const_spec((d, d_in)),
            _const_spec(conv_w.shape),
            _const_spec(lru_conv_w.shape),
            _const_spec((1, W_LRU)),
            _const_spec(wgate.shape),
            _const_spec((1, W_LRU)),
            _const_spec((1, W_LRU)),
            _const_spec((1, W_LRU)),
            _const_spec(w_out.shape),
            *side_in,
        ],
        out_specs=[pl.BlockSpec(memory_space=pl.ANY), *side_out],
        out_shape=[jax.ShapeDtypeStruct((t // SUBLANES, SUBLANES, d), F32), *side_shapes],
        scratch_shapes=[
            pltpu.VMEM((6, seg, SUBLANES, d), F32),
            pltpu.VMEM((tm, d_in), F32),
            pltpu.VMEM((tm, d_in), F32),
            pltpu.VMEM((4, seg, SUBLANES, d), F32),
            pltpu.VMEM(((conv_w.shape[0] - 1) * SUBLANES, W_CONV), F32),
            pltpu.VMEM(((lru_conv_w.shape[0] - 1) * SUBLANES, W_LRU), F32),
            pltpu.VMEM((SUBLANES, W_LRU), F32),
            pltpu.SemaphoreType.DMA((6,)),
            pltpu.SemaphoreType.DMA((4,)),
        ],
        compiler_params=pltpu.CompilerParams(
            dimension_semantics=("arbitrary",), vmem_limit_bytes=VMEM_LIMIT_BYTES),
        name="mixer",
    )(x_in, row(gn), w_in, conv_w, lru_conv_w, row(lru_conv_b), wgate, row(ba), row(bx), row(lam), w_out, *sides)
    return out.reshape(t, d), sides_bf16


def _swiglu_rows(h, wg_ref, wu_ref, wd_ref):
    d_ff = wg_ref.shape[-1]
    acc = jnp.zeros((h.shape[0], wd_ref.shape[-1]), F32)
    for c in range(d_ff // FF_CHUNK):
        cols = slice(c * FF_CHUNK, (c + 1) * FF_CHUNK)
        gate = jnp.dot(h, wg_ref[:, cols], preferred_element_type=F32)
        up = jnp.dot(h, wu_ref[:, cols], preferred_element_type=F32)
        act = (gate * _sigmoid(gate) * up).astype(BF16)
        acc = acc + jnp.dot(act, wd_ref[cols, :], preferred_element_type=F32)
    return acc


def _ffn_kernel(*refs, n_side):
    x_ref, gn_ref, wg_ref, wu_ref, wd_ref = refs[:5]
    side_refs, o_ref, side_out_refs = refs[5:5 + n_side], refs[5 + n_side], refs[6 + n_side:]
    for side_ref, side_out_ref in zip(side_refs, side_out_refs):
        side_out_ref[...] = side_ref[...].astype(BF16)
    x = x_ref[...]
    h = _rms_norm(x, gn_ref[...]).astype(BF16)
    o_ref[...] = x + _swiglu_rows(h, wg_ref, wu_ref, wd_ref)


def _ffn_dense(x2, gn, wg, wu, wd, sides):
    t, d = x2.shape
    tm = TM_FFN
    side_in, side_out, side_shapes = _side_cast_specs(sides, t // tm)
    out, *sides_bf16 = pl.pallas_call(
        functools.partial(_ffn_kernel, n_side=len(sides)),
        grid=(t // tm,),
        in_specs=[
            pl.BlockSpec((tm, d), lambda i: (i, 0)),
            _const_spec((1, d)),
            _const_spec(wg.shape),
            _const_spec(wu.shape),
            _const_spec(wd.shape),
            *side_in,
        ],
        out_specs=[pl.BlockSpec((tm, d), lambda i: (i, 0)), *side_out],
        out_shape=[jax.ShapeDtypeStruct(x2.shape, F32), *side_shapes],
        compiler_params=pltpu.CompilerParams(
            dimension_semantics=("arbitrary",), vmem_limit_bytes=VMEM_LIMIT_BYTES),
        name="ffn_dense",
    )(x2, gn.reshape(1, -1), wg, wu, wd, *sides)
    return out, sides_bf16


def _split_bf16(a):
    hi = a.astype(BF16)
    return hi, (a - hi.astype(F32)).astype(BF16)


def _router_kernel(x_ref, gn_ref, wr_hi_ref, wr_lo_ref, meta_ref, gates_ref, counts_ref, run):
    i = pl.program_id(0)
    tm = x_ref.shape[0]

    @pl.when(i == 0)
    def _():
        run[...] = jnp.zeros((1, LANES), F32)

    h_hi, h_lo = _split_bf16(_rms_norm(x_ref[...], gn_ref[...]))
    logits = (jnp.dot(h_hi, wr_hi_ref[...], preferred_element_type=F32)
              + jnp.dot(h_lo, wr_hi_ref[...], preferred_element_type=F32)
              + jnp.dot(h_hi, wr_lo_ref[...], preferred_element_type=F32))
    lane = lax.broadcasted_iota(I32, (tm, LANES), 1)
    neg_inf = jnp.float32(-jnp.inf)
    logits = jnp.where(lane < N_EXPERTS, logits, neg_inf)
    m1 = jnp.max(logits, axis=-1, keepdims=True)
    i1 = jnp.min(jnp.where(logits == m1, lane, LANES), axis=-1, keepdims=True)
    rest = jnp.where(lane == i1, neg_inf, logits)
    m2 = jnp.max(rest, axis=-1, keepdims=True)
    i2 = jnp.min(jnp.where(rest == m2, lane, LANES), axis=-1, keepdims=True)
    e = jnp.exp(m2 - m1)
    g1 = 1.0 / (1.0 + e)
    g2 = e * g1

    sel1 = lane == i1
    sel2 = lane == i2
    hot = jnp.where(sel1 | sel2, 1.0, 0.0)
    rows = lax.broadcasted_iota(I32, (tm, tm), 0)
    cols = lax.broadcasted_iota(I32, (tm, tm), 1)
    strict_lower = jnp.where(cols < rows, 1.0, 0.0).astype(BF16)
    before = jnp.dot(strict_lower, hot.astype(BF16), preferred_element_type=F32) + run[...]
    rank1 = jnp.sum(jnp.where(sel1, before, 0.0), axis=-1, keepdims=True)
    rank2 = jnp.sum(jnp.where(sel2, before, 0.0), axis=-1, keepdims=True)
    run[...] = run[...] + jnp.sum(hot, axis=0, keepdims=True)

    meta = jnp.where(lane == 0, i1, jnp.where(lane == 1, i2, jnp.where(
        lane == 2, rank1.astype(I32), jnp.where(lane == 3, rank2.astype(I32), 0))))
    meta_ref[...] = jnp.transpose(meta)[0:SUBLANES, :]
    gates_ref[...] = jnp.where(lane == 0, g1, jnp.where(lane == 1, g2, 0.0))
    counts_ref[...] = jnp.broadcast_to(run[...], (SUBLANES, LANES))


def _router(x2, gn, w_router):
    t, d = x2.shape
    tm = TM_ROUTE
    wr_hi, wr_lo = _split_bf16(jnp.zeros((d, LANES), F32).at[:, :N_EXPERTS].set(w_router))
    return pl.pallas_call(
        _router_kernel,
        grid=(t // tm,),
        in_specs=[
            pl.BlockSpec((tm, d), lambda i: (i, 0)),
            _const_spec((1, d)),
            _const_spec((d, LANES)),
            _const_spec((d, LANES)),
        ],
        out_specs=[
            pl.BlockSpec((SUBLANES, tm), lambda i: (0, i)),
            pl.BlockSpec((tm, LANES), lambda i: (i, 0)),
            pl.BlockSpec((SUBLANES, LANES), lambda i: (0, 0)),
        ],
        out_shape=[
            jax.ShapeDtypeStruct((SUBLANES, t), I32),
            jax.ShapeDtypeStruct((t, LANES), F32),
            jax.ShapeDtypeStruct((SUBLANES, LANES), F32),
        ],
        scratch_shapes=[pltpu.VMEM((1, LANES), F32)],
        compiler_params=pltpu.CompilerParams(
            dimension_semantics=("arbitrary",), vmem_limit_bytes=VMEM_LIMIT_BYTES),
        name="router",
    )(x2, gn.reshape(1, -1), wr_hi, wr_lo)


def _row_copy(src_ref, src_row, dst_ref, dst_row, sem):
    return pltpu.make_async_copy(src_ref.at[pl.ds(src_row, 1), :], dst_ref.at[pl.ds(dst_row, 1), :], sem)


def _dispatch_kernel(pad_ref, pos_ref, x_ref, xs_hbm, ztile, sem, zsem):
    i = pl.program_id(0)
    tile_rows = ztile.shape[0]

    @pl.when(i == 0)
    def _():
        ztile[...] = jnp.zeros(ztile.shape, F32)
        for e in range(N_EXPERTS):
            start = pad_ref[0, e]
            n = pad_ref[1, e]

            def fill(q, c):
                _row_copy(ztile, 0, xs_hbm, start + q, zsem).start()
                return c

            def drain(q, c):
                _row_copy(ztile, 0, xs_hbm, start, zsem).wait()
                return c

            lax.fori_loop(0, n, fill, 0)
            lax.fori_loop(0, n, drain, 0)

        def tail_copy(q):
            r0 = pl.multiple_of(q * tile_rows, tile_rows)
            return pltpu.make_async_copy(ztile, xs_hbm.at[pl.ds(r0, tile_rows), :], zsem)

        def fill_tail(q, c):
            tail_copy(q).start()
            return c

        def drain_tail(q, c):
            tail_copy(q).wait()
            return c

        n_valid = pad_ref[2, 0]
        n_tiles = xs_hbm.shape[0] // tile_rows
        lax.fori_loop(n_valid, n_tiles, fill_tail, 0)
        lax.fori_loop(n_valid, n_tiles, drain_tail, 0)

    def issue(q, c):
        for s in range(SUBLANES):
            for k in range(TOP_K):
                dst = pos_ref[0, 0, TOP_K * (q * SUBLANES + s) + k]
                pltpu.make_async_copy(x_ref.at[q, pl.ds(s, 1), :], xs_hbm.at[pl.ds(dst, 1), :],
                                      sem.at[k]).start(priority=k)
        return c

    n_groups = x_ref.shape[0]
    lax.fori_loop(0, n_groups, issue, 0)
    for k in range(TOP_K):
        for _ in range(n_groups * SUBLANES // tile_rows):
            pltpu.make_async_copy(ztile, xs_hbm.at[pl.ds(0, tile_rows), :], sem.at[k]).wait()


def _dispatch(x2, pos, pad_info, n_rows):
    t, d = x2.shape
    tm = TM_DISP
    pos3 = pos.reshape(t // tm, 1, TOP_K * tm)
    grid_spec = pltpu.PrefetchScalarGridSpec(
        num_scalar_prefetch=1,
        grid=(t // tm,),
        in_specs=[
            pl.BlockSpec((1, 1, TOP_K * tm), lambda i, pad: (i, 0, 0), memory_space=pltpu.SMEM),
            pl.BlockSpec((tm // SUBLANES, SUBLANES, d), lambda i, pad: (i, 0, 0)),
        ],
        out_specs=pl.BlockSpec(memory_space=pl.ANY),
        scratch_shapes=[
            pltpu.VMEM((TM_EXP, d), F32),
            pltpu.SemaphoreType.DMA((TOP_K,)),
            pltpu.SemaphoreType.DMA(()),
        ],
    )
    return pl.pallas_call(
        _dispatch_kernel,
        grid_spec=grid_spec,
        out_shape=jax.ShapeDtypeStruct((n_rows, d), F32),
        compiler_params=pltpu.CompilerParams(
            dimension_semantics=("arbitrary",), vmem_limit_bytes=VMEM_LIMIT_BYTES),
        name="dispatch",
    )(pad_info, pos3, x2.reshape(t // SUBLANES, SUBLANES, d))


def _expert_kernel(texp_ref, nvalid_ref, xs_ref, gn_ref, wg_ref, wu_ref, wd_ref, ys_ref):
    i = pl.program_id(0)

    @pl.when(i < nvalid_ref[0])
    def _():
        h = _rms_norm(xs_ref[...], gn_ref[...]).astype(BF16)
        ys_ref[...] = _swiglu_rows(h, wg_ref, wu_ref, wd_ref)

    @pl.when(i >= nvalid_ref[0])
    def _():
        ys_ref[...] = jnp.zeros(ys_ref.shape, F32)


def _experts(xs, gn, wg, wu, wd, tile_expert, n_valid):
    n_rows, d = xs.shape
    tm = TM_EXP
    d_ff = wg.shape[-1]
    last = lambda i, nv: jnp.maximum(jnp.minimum(i, nv[0] - 1), 0)
    tile = lambda i, te, nv: (last(i, nv), 0)
    expert = lambda i, te, nv: (te[last(i, nv)], 0, 0)
    grid_spec = pltpu.PrefetchScalarGridSpec(
        num_scalar_prefetch=2,
        grid=(n_rows // tm,),
        in_specs=[
            pl.BlockSpec((tm, d), tile),
            pl.BlockSpec((1, d), lambda i, te, nv: (0, 0)),
            pl.BlockSpec((None, d, d_ff), expert),
            pl.BlockSpec((None, d, d_ff), expert),
            pl.BlockSpec((None, d_ff, d), expert),
        ],
        out_specs=pl.BlockSpec((tm, d), lambda i, te, nv: (i, 0)),
    )
    return pl.pallas_call(
        _expert_kernel,
        grid_spec=grid_spec,
        out_shape=jax.ShapeDtypeStruct((n_rows, d), F32),
        compiler_params=pltpu.CompilerParams(
            dimension_semantics=("arbitrary",), vmem_limit_bytes=VMEM_LIMIT_BYTES),
        name="experts",
    )(tile_expert, n_valid, xs, gn.reshape(1, -1), wg.astype(BF16), wu.astype(BF16), wd.astype(BF16))


def _combine_kernel(pos0_ref, pos_next_ref, x_ref, gates_ref, gf_ref, ys_rows, ys_hbm, o_ref, buf, sem):
    i = pl.program_id(0)
    n_groups = buf.shape[2]
    tm = n_groups * SUBLANES
    d = buf.shape[4]

    def gather(pos_ref, slot):
        def issue(q, c):
            for s in range(SUBLANES):
                for k in range(TOP_K):
                    src = pos_ref[0, 0, TOP_K * (q * SUBLANES + s) + k]
                    pltpu.make_async_copy(ys_rows.at[pl.ds(src, 1), :],
                                          buf.at[slot, k, q, pl.ds(s, 1), :], sem.at[slot, k]).start(priority=k)
            return c

        lax.fori_loop(0, n_groups, issue, 0)

    @pl.when(i == 0)
    def _():
        gather(pos0_ref, 0)

    @pl.when(i + 1 < pl.num_programs(0))
    def _():
        gather(pos_next_ref, (i + 1) % 2)

    slot = i % 2
    for k in range(TOP_K):
        pltpu.make_async_copy(ys_hbm.at[pl.ds(0, n_groups)], buf.at[slot, k], sem.at[slot, k]).wait()
    gates = gates_ref[...]
    y = (x_ref[...] + gates[:, 0:1] * buf[slot, 0].reshape(tm, d) + gates[:, 1:2] * buf[slot, 1].reshape(tm, d))
    o_ref[...] = _rms_norm(y, gf_ref[...])


def _combine(x2, gates, pos, ys, g_final):
    t, d = x2.shape
    tm = TM_DISP
    n_steps = t // tm
    pos3 = pos.reshape(n_steps, 1, TOP_K * tm)
    pos_block = (1, 1, TOP_K * tm)
    return pl.pallas_call(
        _combine_kernel,
        grid=(n_steps,),
        in_specs=[
            pl.BlockSpec(pos_block, lambda i: (0, 0, 0), memory_space=pltpu.SMEM),
            pl.BlockSpec(pos_block, lambda i: (jnp.minimum(i + 1, n_steps - 1), 0, 0), memory_space=pltpu.SMEM),
            pl.BlockSpec((tm, d), lambda i: (i, 0)),
            pl.BlockSpec((tm, LANES), lambda i: (i, 0)),
            _const_spec((1, d)),
            pl.BlockSpec(memory_space=pl.ANY),
            pl.BlockSpec(memory_space=pl.ANY),
        ],
        out_specs=pl.BlockSpec((tm, d), lambda i: (i, 0)),
        out_shape=jax.ShapeDtypeStruct(x2.shape, F32),
        scratch_shapes=[pltpu.VMEM((2, TOP_K, tm // SUBLANES, SUBLANES, d), F32),
                        pltpu.SemaphoreType.DMA((2, TOP_K))],
        compiler_params=pltpu.CompilerParams(
            dimension_semantics=("arbitrary",), vmem_limit_bytes=VMEM_LIMIT_BYTES),
        name="combine",
    )(pos3, pos3, x2, gates, g_final.reshape(1, -1), ys, ys.reshape(ys.shape[0] // SUBLANES, SUBLANES, d))


def _moe_layer(x2, gn, w_router, wg, wu, wd, g_final):
    t, _ = x2.shape
    tm = TM_EXP
    n_tiles = (TOP_K * t) // tm + N_EXPERTS
    meta, gates, counts = _router(x2, gn, w_router)

    counts = counts[0, :N_EXPERTS].astype(I32)
    padded = ((counts + tm - 1) // tm) * tm
    ends = jnp.cumsum(padded)
    starts = ends - padded
    idx = meta[0:TOP_K, :]
    rank = meta[TOP_K:2 * TOP_K, :]
    pos = (starts[idx] + rank).T.reshape(-1)
    tile_start = jnp.arange(n_tiles, dtype=I32) * tm
    tile_expert = jnp.minimum(
        jnp.sum((tile_start[:, None] >= ends[None, :]).astype(I32), axis=1), N_EXPERTS - 1).astype(I32)
    n_valid = (ends[-1:] // tm).astype(I32)
    pad_info = jnp.stack([starts + counts, padded - counts, jnp.broadcast_to(n_valid, (N_EXPERTS,))]).astype(I32)

    xs = _dispatch(x2, pos, pad_info, n_tiles * tm)
    ys = _experts(xs, gn, wg, wu, wd, tile_expert, n_valid)
    return _combine(x2, gates, pos, ys, g_final)


def kernel(x, norm_mix, norm_ffn, norm_final, w_in, conv_w, lru_conv_w, lru_conv_b, lru_wa, lru_ba, lru_wx,
           lru_bx, lru_lambda, w_out, ffn_w_gate, ffn_w_up, ffn_w_down, w_router, moe_w_gate, moe_w_up,
           moe_w_down):
    bsz, seq, d = x.shape
    depth = w_in.shape[0]
    assert depth == 2, "layer 0 uses the dense FFN, layer 1 the routed FFN followed by the final norm"
    x2 = x.reshape(bsz * seq, d)
    def mixer(x2, l, w_in_l, w_out_l, sides):
        return _mixer(x2, seq, l == 0, norm_mix[l], w_in_l, conv_w[l], lru_conv_w[l], lru_conv_b[l], lru_wa[l],
                      lru_ba[l], lru_wx[l], lru_bx[l], lru_lambda[l], w_out_l, sides)

    flat = lambda w: w.reshape(-1, w.shape[-1])
    x2, (moe_gate, ffn_gate, ffn_up, ffn_down) = mixer(
        x2, 0, w_in[0].astype(BF16), w_out[0].astype(BF16),
        [flat(moe_w_gate[0]), ffn_w_gate[0], ffn_w_up[0], ffn_w_down[0]])
    x2, (moe_down, w_in_1, w_out_1) = _ffn_dense(
        x2, norm_ffn[0], ffn_gate, ffn_up, ffn_down, [flat(moe_w_down[0]), w_in[1], w_out[1]])
    x2, (moe_up,) = mixer(x2, 1, w_in_1, w_out_1, [flat(moe_w_up[0])])
    x2 = _moe_layer(x2, norm_ffn[1], w_router[0], moe_gate.reshape(moe_w_gate[0].shape),
                    moe_up.reshape(moe_w_up[0].shape), moe_down.reshape(moe_w_down[0].shape), norm_final)
    return _from_segment_major(x2, TM_MIX).reshape(bsz, seq, d)
```

```python
import functools

import jax
import jax.numpy as jnp
from jax import lax
from jax.experimental import pallas as pl
from jax.experimental.pallas import tpu as pltpu

F32 = jnp.float32
BF16 = jnp.bfloat16
I32 = jnp.int32

EPS = 1e-6
RG_C = 8.0
W_CONV = 512
W_LRU = 512
N_EXPERTS = 8
TOP_K = 2

SUBLANES = 8
LANES = 128
MXU_DIM = 256
VMEM_LIMIT_BYTES = 56 * 1024 * 1024

TM_MIX = 256
TM_FFN = 512
FF_CHUNK = 512
TM_ROUTE = 512
TM_DISP = 512
TM_EXP = 512


def _const_spec(shape):
    nd = len(shape)
    return pl.BlockSpec(shape, lambda *_: (0,) * nd, pipeline_mode=pl.Buffered(1))


def _rms_norm(x, g):
    ms = jnp.mean(x * x, axis=-1, keepdims=True)
    return x * lax.rsqrt(ms + EPS) * g


def _sigmoid(x):
    return 1.0 / (1.0 + jnp.exp(-x))


def _gelu_tanh(x):
    c = 0.7978845608028654
    return 0.5 * x * (1.0 + jnp.tanh(c * (x + 0.044715 * (x * x * x))))


def _segment_copies(hbm, tile, buf, slot, sem, to_hbm):
    seg = buf.shape[1]
    copies = []
    for i in range(SUBLANES):
        rows = hbm.at[pl.ds(tile * (seg * SUBLANES) + i * seg, seg), :]
        vm = buf.at[slot, :, i, :]
        copies.append(pltpu.make_async_copy(vm, rows, sem.at[slot]) if to_hbm
                      else pltpu.make_async_copy(rows, vm, sem.at[slot]))
    return copies


def _segment_halo(cur, prev, n):
    first = lax.broadcasted_iota(I32, (SUBLANES, cur.shape[1]), 0) == 0
    out = []
    for k in range(n):
        rows = slice(k * SUBLANES, (k + 1) * SUBLANES)
        out.append(jnp.where(first, pltpu.roll(prev[rows, :], 1, axis=0), pltpu.roll(cur[rows, :], 1, axis=0)))
    return jnp.concatenate(out, axis=0)


def _mixer_kernel(*refs, n_side, n_tiles, tiles_per_seq, time_ordered_input, time_ordered_output):
    (x_hbm, gn_ref, win_ref, cw_ref, lcw_ref, lcb_ref, wgate_ref, ba_ref, bx_ref, lam_ref, wout_ref), refs = (
        refs[:11], refs[11:])
    side_refs, o_hbm, side_out_refs = refs[:n_side], refs[n_side], refs[n_side + 1:2 * n_side + 1]
    xbuf, p_even, p_odd, obuf, ztail, utail, hcarry, xsem, osem = refs[2 * n_side + 1:]
    k = pl.program_id(0)
    last_step = pl.num_programs(0) - 1
    seg = xbuf.shape[1]
    tm = seg * SUBLANES
    d = xbuf.shape[3]
    n_x = xbuf.shape[0]
    t_even = 2 * k
    t_odd = 2 * k + 1

    def fetch(tile):
        slot = tile % n_x
        if time_ordered_input:
            return _segment_copies(x_hbm, tile, xbuf, slot, xsem, False)
        return [pltpu.make_async_copy(x_hbm.at[pl.ds(tile * seg, seg)], xbuf.at[slot], xsem.at[slot])]

    n_o = obuf.shape[0]

    def out_slot(tile):
        return (tile + n_o) % n_o

    def write_back(tile):
        slot = out_slot(tile)
        if time_ordered_output:
            return _segment_copies(o_hbm, tile, obuf, slot, osem, True)
        return [pltpu.make_async_copy(obuf.at[slot], o_hbm.at[pl.ds(tile * seg, seg)], osem.at[slot])]

    @pl.when(k == 0)
    def _():
        for c in fetch(0) + fetch(1):
            c.start()
        xbuf[n_x - 1] = jnp.zeros(xbuf.shape[1:], F32)
        p_odd[...] = jnp.zeros(p_odd.shape, F32)
        ztail[...] = jnp.zeros(ztail.shape, F32)
        utail[...] = jnp.zeros(utail.shape, F32)
        hcarry[...] = jnp.zeros(hcarry.shape, F32)

    @pl.when(k < last_step)
    def _():
        for c in fetch(t_even) + fetch(t_odd):
            c.wait()

    @pl.when(k + 1 < last_step)
    def _():
        for c in fetch(t_even + 2) + fetch(t_odd + 2):
            c.start()

    @pl.when(k >= 3)
    def _():
        for c in write_back(t_even - 1 - n_o):
            c.wait()

    @pl.when(k >= 2)
    def _():
        for c in write_back(t_even - n_o):
            c.wait()

    def project(tile, p_ref):
        xa = xbuf[jnp.minimum(tile, n_tiles - 1) % n_x].reshape(tm, d)
        ha = _rms_norm(xa, gn_ref[...]).astype(BF16)
        p_ref[...] = jnp.dot(ha, win_ref[...], preferred_element_type=F32)

    def finish(tile, p_ref, state):
        z_tail, u_tail, h_last = state
        fresh = (tile + tiles_per_seq) % tiles_per_seq == 0
        z_tail = jnp.where(fresh, 0.0, z_tail)
        u_tail = jnp.where(fresh, 0.0, u_tail)
        h_last = jnp.where(fresh, 0.0, h_last)
        c_gate = p_ref[:, 0:W_CONV]
        b_gate = p_ref[:, W_CONV:2 * W_CONV]
        v = p_ref[:, 2 * W_CONV:3 * W_CONV]
        u = p_ref[:, 3 * W_CONV:3 * W_CONV + W_LRU]
        g = p_ref[:, 3 * W_CONV + W_LRU:]

        z = c_gate * v
        nz = z_tail.shape[0]
        z_new_tail = z[tm - nz:, :]
        zx = jnp.concatenate([_segment_halo(z_new_tail, z_tail, nz // SUBLANES), z], axis=0)
        cw = cw_ref[...]
        conv = cw[2:3, :] * z + cw[1:2, :] * zx[nz - SUBLANES:nz - SUBLANES + tm, :] + cw[0:1, :] * zx[0:tm, :]
        y_a = b_gate * conv

        nu = u_tail.shape[0]
        u_new_tail = u[tm - nu:, :]
        ux = jnp.concatenate([_segment_halo(u_new_tail, u_tail, nu // SUBLANES), u], axis=0)
        lw = lcw_ref[...]
        uc = (lw[3:4, :] * u + lw[2:3, :] * ux[2 * SUBLANES:2 * SUBLANES + tm, :]
              + lw[1:2, :] * ux[SUBLANES:SUBLANES + tm, :] + lw[0:1, :] * ux[0:tm, :] + lcb_ref[...])

        ucb = uc.astype(BF16)
        pre0 = jnp.dot(ucb[:, 0:MXU_DIM], wgate_ref[0], preferred_element_type=F32)
        pre1 = jnp.dot(ucb[:, MXU_DIM:], wgate_ref[1], preferred_element_type=F32)
        pre_a = jnp.concatenate([pre0[:, :MXU_DIM], pre1[:, :MXU_DIM]], axis=1) + ba_ref[...]
        pre_x = jnp.concatenate([pre0[:, MXU_DIM:], pre1[:, MXU_DIM:]], axis=1) + bx_ref[...]
        r = _sigmoid(pre_a)
        i_gate = _sigmoid(pre_x)
        neg_lam = -lam_ref[...]
        softplus = jnp.maximum(neg_lam, 0.0) + jnp.log1p(jnp.exp(-jnp.abs(neg_lam)))
        log_a = (-RG_C) * r * softplus
        a = jnp.exp(log_a)
        th = jnp.tanh(log_a)
        b = jnp.sqrt((-2.0 * th) / (1.0 - th)) * (i_gate * uc)

        a3 = a.reshape(seg, SUBLANES, W_LRU)
        b3 = b.reshape(seg, SUBLANES, W_LRU)
        prod = [a3[0]]
        part = [b3[0]]
        for j in range(1, seg):
            prod.append(a3[j] * prod[j - 1])
            part.append(a3[j] * part[j - 1] + b3[j])
        init_rows = []
        for i in range(SUBLANES):
            init_rows.append(h_last)
            h_last = prod[seg - 1][i:i + 1, :] * h_last + part[seg - 1][i:i + 1, :]
        init = jnp.concatenate(init_rows, axis=0)
        hs = jnp.stack([part[j] + prod[j] * init for j in range(seg)], axis=0).reshape(tm, W_LRU)

        y_b = hs * _gelu_tanh(g)
        y = jnp.concatenate([y_a, y_b], axis=1).astype(BF16)
        xc = xbuf[(tile + n_x) % n_x].reshape(tm, d)
        out = xc + jnp.dot(y, wout_ref[...], preferred_element_type=F32)
        return out.reshape(seg, SUBLANES, d), (z_new_tail, u_new_tail, h_last)

    for side_ref, side_out_ref in zip(side_refs, side_out_refs):
        side_out_ref[...] = side_ref[...].astype(BF16)
    state = (ztail[...], utail[...], hcarry[0:1, :])
    project(t_even, p_even)
    obuf[out_slot(t_even - 1)], state = finish(t_even - 1, p_odd, state)
    project(t_odd, p_odd)
    obuf[out_slot(t_even)], state = finish(t_even, p_even, state)
    ztail[...], utail[...] = state[0], state[1]
    hcarry[...] = jnp.broadcast_to(state[2], hcarry.shape)

    @pl.when(k >= 1)
    def _():
        for c in write_back(t_even - 1):
            c.start()

    @pl.when(k < last_step)
    def _():
        for c in write_back(t_even):
            c.start()

    @pl.when(k == last_step)
    def _():
        for c in write_back(t_even - 3) + write_back(t_even - 2) + write_back(t_even - 1):
            c.wait()


def _block_diag_halves(w):
    nh, dh, _ = w.shape
    per = MXU_DIM // dh
    w4 = w.reshape(nh // per, per, dh, dh)
    eye = jnp.eye(per, dtype=w.dtype)
    return jnp.einsum("kaij,ab->kaibj", w4, eye).reshape(nh // per, MXU_DIM, MXU_DIM)


def _side_cast_specs(sides, n_blocks):
    index = lambda i: (jnp.minimum(i, n_blocks - 1), 0)
    specs = [pl.BlockSpec((s.shape[0] // n_blocks, s.shape[1]), index) for s in sides]
    return specs, list(specs), [jax.ShapeDtypeStruct(s.shape, BF16) for s in sides]


def _mixer(x2, seq, time_ordered_input, time_ordered_output, gn, w_in, conv_w, lru_conv_w, lru_conv_b, wa, ba,
           wx, bx, lam, w_out, sides):
    t, d = x2.shape
    d_in = w_in.shape[1]
    tm = TM_MIX
    seg = tm // SUBLANES
    n_tiles = t // tm
    assert n_tiles % 2 == 0 and (seq // tm) % 2 == 0
    wgate = jnp.concatenate([_block_diag_halves(wa), _block_diag_halves(wx)], axis=-1).astype(BF16)
    row = lambda a: a.reshape(1, -1)
    x_in = x2 if time_ordered_input else x2.reshape(t // SUBLANES, SUBLANES, d)
    side_in, side_out, side_shapes = _side_cast_specs(sides, n_tiles // 2)
    out, *sides_bf16 = pl.pallas_call(
        functools.partial(_mixer_kernel, n_side=len(sides), n_tiles=n_tiles, tiles_per_seq=seq // tm,
                          time_ordered_input=time_ordered_input, time_ordered_output=time_ordered_output),
        grid=(n_tiles // 2 + 1,),
        in_specs=[
            pl.BlockSpec(memory_space=pl.ANY),
            _const_spec((1, d)),
            _const_spec((d, d_in)),
            _const_spec(conv_w.shape),
            _const_spec(lru_conv_w.shape),
            _const_spec((1, W_LRU)),
            _const_spec(wgate.shape),
            _const_spec((1, W_LRU)),
            _const_spec((1, W_LRU)),
            _const_spec((1, W_LRU)),
            _const_spec(w_out.shape),
            *side_in,
        ],
        out_specs=[pl.BlockSpec(memory_space=pl.ANY), *side_out],
        out_shape=[jax.ShapeDtypeStruct((t, d) if time_ordered_output else (t // SUBLANES, SUBLANES, d), F32),
                   *side_shapes],
        scratch_shapes=[
            pltpu.VMEM((6, seg, SUBLANES, d), F32),
            pltpu.VMEM((tm, d_in), F32),
            pltpu.VMEM((tm, d_in), F32),
            pltpu.VMEM((4, seg, SUBLANES, d), F32),
            pltpu.VMEM(((conv_w.shape[0] - 1) * SUBLANES, W_CONV), F32),
            pltpu.VMEM(((lru_conv_w.shape[0] - 1) * SUBLANES, W_LRU), F32),
            pltpu.VMEM((SUBLANES, W_LRU), F32),
            pltpu.SemaphoreType.DMA((6,)),
            pltpu.SemaphoreType.DMA((4,)),
        ],
        compiler_params=pltpu.CompilerParams(
            dimension_semantics=("arbitrary",), vmem_limit_bytes=VMEM_LIMIT_BYTES),
        name="mixer",
    )(x_in, row(gn), w_in, conv_w, lru_conv_w, row(lru_conv_b), wgate, row(ba), row(bx), row(lam), w_out, *sides)
    return out.reshape(t, d), sides_bf16


def _swiglu_rows(h, wg_ref, wu_ref, wd_ref):
    d_ff = wg_ref.shape[-1]
    acc = jnp.zeros((h.shape[0], wd_ref.shape[-1]), F32)
    for c in range(d_ff // FF_CHUNK):
        cols = slice(c * FF_CHUNK, (c + 1) * FF_CHUNK)
        gate = jnp.dot(h, wg_ref[:, cols], preferred_element_type=F32)
        up = jnp.dot(h, wu_ref[:, cols], preferred_element_type=F32)
        act = (gate * _sigmoid(gate) * up).astype(BF16)
        acc = acc + jnp.dot(act, wd_ref[cols, :], preferred_element_type=F32)
    return acc


def _ffn_kernel(*refs, n_side):
    x_ref, gn_ref, wg_ref, wu_ref, wd_ref = refs[:5]
    side_refs, o_ref, side_out_refs = refs[5:5 + n_side], refs[5 + n_side], refs[6 + n_side:]
    for side_ref, side_out_ref in zip(side_refs, side_out_refs):
        side_out_ref[...] = side_ref[...].astype(BF16)
    x = x_ref[...]
    h = _rms_norm(x, gn_ref[...]).astype(BF16)
    o_ref[...] = x + _swiglu_rows(h, wg_ref, wu_ref, wd_ref)


def _ffn_dense(x2, gn, wg, wu, wd, sides):
    t, d = x2.shape
    tm = TM_FFN
    side_in, side_out, side_shapes = _side_cast_specs(sides, t // tm)
    out, *sides_bf16 = pl.pallas_call(
        functools.partial(_ffn_kernel, n_side=len(sides)),
        grid=(t // tm,),
        in_specs=[
            pl.BlockSpec((tm, d), lambda i: (i, 0)),
            _const_spec((1, d)),
            _const_spec(wg.shape),
            _const_spec(wu.shape),
            _const_spec(wd.shape),
            *side_in,
        ],
        out_specs=[pl.BlockSpec((tm, d), lambda i: (i, 0)), *side_out],
        out_shape=[jax.ShapeDtypeStruct(x2.shape, F32), *side_shapes],
        compiler_params=pltpu.CompilerParams(
            dimension_semantics=("arbitrary",), vmem_limit_bytes=VMEM_LIMIT_BYTES),
        name="ffn_dense",
    )(x2, gn.reshape(1, -1), wg, wu, wd, *sides)
    return out, sides_bf16


def _split_bf16(a):
    hi = a.astype(BF16)
    return hi, (a - hi.astype(F32)).astype(BF16)


def _router_kernel(x_ref, gn_ref, wr_hi_ref, wr_lo_ref, meta_ref, gates_ref, counts_ref, run):
    i = pl.program_id(0)
    tm = x_ref.shape[0]

    @pl.when(i == 0)
    def _():
        run[...] = jnp.zeros((1, LANES), F32)

    h_hi, h_lo = _split_bf16(_rms_norm(x_ref[...], gn_ref[...]))
    logits = (jnp.dot(h_hi, wr_hi_ref[...], preferred_element_type=F32)
              + jnp.dot(h_lo, wr_hi_ref[...], preferred_element_type=F32)
              + jnp.dot(h_hi, wr_lo_ref[...], preferred_element_type=F32))
    lane = lax.broadcasted_iota(I32, (tm, LANES), 1)
    neg_inf = jnp.float32(-jnp.inf)
    logits = jnp.where(lane < N_EXPERTS, logits, neg_inf)
    m1 = jnp.max(logits, axis=-1, keepdims=True)
    i1 = jnp.min(jnp.where(logits == m1, lane, LANES), axis=-1, keepdims=True)
    rest = jnp.where(lane == i1, neg_inf, logits)
    m2 = jnp.max(rest, axis=-1, keepdims=True)
    i2 = jnp.min(jnp.where(rest == m2, lane, LANES), axis=-1, keepdims=True)
    e = jnp.exp(m2 - m1)
    g1 = 1.0 / (1.0 + e)
    g2 = e * g1

    sel1 = lane == i1
    sel2 = lane == i2
    hot = jnp.where(sel1 | sel2, 1.0, 0.0)
    rows = lax.broadcasted_iota(I32, (tm, tm), 0)
    cols = lax.broadcasted_iota(I32, (tm, tm), 1)
    strict_lower = jnp.where(cols < rows, 1.0, 0.0).astype(BF16)
    before = jnp.dot(strict_lower, hot.astype(BF16), preferred_element_type=F32) + run[...]
    rank1 = jnp.sum(jnp.where(sel1, before, 0.0), axis=-1, keepdims=True)
    rank2 = jnp.sum(jnp.where(sel2, before, 0.0), axis=-1, keepdims=True)
    run[...] = run[...] + jnp.sum(hot, axis=0, keepdims=True)

    meta = jnp.where(lane == 0, i1, jnp.where(lane == 1, i2, jnp.where(
        lane == 2, rank1.astype(I32), jnp.where(lane == 3, rank2.astype(I32), 0))))
    meta_ref[...] = jnp.transpose(meta)[0:SUBLANES, :]
    gates_ref[...] = jnp.where(lane == 0, g1, jnp.where(lane == 1, g2, 0.0))
    counts_ref[...] = jnp.broadcast_to(run[...], (SUBLANES, LANES))


def _router(x2, gn, w_router):
    t, d = x2.shape
    tm = TM_ROUTE
    wr_hi, wr_lo = _split_bf16(jnp.zeros((d, LANES), F32).at[:, :N_EXPERTS].set(w_router))
    return pl.pallas_call(
        _router_kernel,
        grid=(t // tm,),
        in_specs=[
            pl.BlockSpec((tm, d), lambda i: (i, 0)),
            _const_spec((1, d)),
            _const_spec((d, LANES)),
            _const_spec((d, LANES)),
        ],
        out_specs=[
            pl.BlockSpec((SUBLANES, tm), lambda i: (0, i)),
            pl.BlockSpec((tm, LANES), lambda i: (i, 0)),
            pl.BlockSpec((SUBLANES, LANES), lambda i: (0, 0)),
        ],
        out_shape=[
            jax.ShapeDtypeStruct((SUBLANES, t), I32),
            jax.ShapeDtypeStruct((t, LANES), F32),
            jax.ShapeDtypeStruct((SUBLANES, LANES), F32),
        ],
        scratch_shapes=[pltpu.VMEM((1, LANES), F32)],
        compiler_params=pltpu.CompilerParams(
            dimension_semantics=("arbitrary",), vmem_limit_bytes=VMEM_LIMIT_BYTES),
        name="router",
    )(x2, gn.reshape(1, -1), wr_hi, wr_lo)


def _row_copy(src_ref, src_row, dst_ref, dst_row, sem):
    return pltpu.make_async_copy(src_ref.at[pl.ds(src_row, 1), :], dst_ref.at[pl.ds(dst_row, 1), :], sem)


def _dispatch_kernel(pad_ref, pos_ref, x_ref, xs_hbm, ztile, sem, zsem):
    i = pl.program_id(0)
    tile_rows = ztile.shape[0]

    @pl.when(i == 0)
    def _():
        ztile[...] = jnp.zeros(ztile.shape, F32)
        for e in range(N_EXPERTS):
            start = pad_ref[0, e]
            n = pad_ref[1, e]

            def fill(q, c):
                _row_copy(ztile, 0, xs_hbm, start + q, zsem).start()
                return c

            def drain(q, c):
                _row_copy(ztile, 0, xs_hbm, start, zsem).wait()
                return c

            lax.fori_loop(0, n, fill, 0)
            lax.fori_loop(0, n, drain, 0)

        def tail_copy(q):
            r0 = pl.multiple_of(q * tile_rows, tile_rows)
            return pltpu.make_async_copy(ztile, xs_hbm.at[pl.ds(r0, tile_rows), :], zsem)

        def fill_tail(q, c):
            tail_copy(q).start()
            return c

        def drain_tail(q, c):
            tail_copy(q).wait()
            return c

        n_valid = pad_ref[2, 0]
        n_tiles = xs_hbm.shape[0] // tile_rows
        lax.fori_loop(n_valid, n_tiles, fill_tail, 0)
        lax.fori_loop(n_valid, n_tiles, drain_tail, 0)

    def issue(q, c):
        for s in range(SUBLANES):
            for k in range(TOP_K):
                dst = pos_ref[0, 0, TOP_K * (q * SUBLANES + s) + k]
                pltpu.make_async_copy(x_ref.at[q, pl.ds(s, 1), :], xs_hbm.at[pl.ds(dst, 1), :],
                                      sem.at[k]).start(priority=k)
        return c

    n_groups = x_ref.shape[0]
    lax.fori_loop(0, n_groups, issue, 0)
    for k in range(TOP_K):
        for _ in range(n_groups * SUBLANES // tile_rows):
            pltpu.make_async_copy(ztile, xs_hbm.at[pl.ds(0, tile_rows), :], sem.at[k]).wait()


def _dispatch(x2, pos, pad_info, n_rows):
    t, d = x2.shape
    tm = TM_DISP
    pos3 = pos.reshape(t // tm, 1, TOP_K * tm)
    grid_spec = pltpu.PrefetchScalarGridSpec(
        num_scalar_prefetch=1,
        grid=(t // tm,),
        in_specs=[
            pl.BlockSpec((1, 1, TOP_K * tm), lambda i, pad: (i, 0, 0), memory_space=pltpu.SMEM),
            pl.BlockSpec((tm // SUBLANES, SUBLANES, d), lambda i, pad: (i, 0, 0)),
        ],
        out_specs=pl.BlockSpec(memory_space=pl.ANY),
        scratch_shapes=[
            pltpu.VMEM((TM_EXP, d), F32),
            pltpu.SemaphoreType.DMA((TOP_K,)),
            pltpu.SemaphoreType.DMA(()),
        ],
    )
    return pl.pallas_call(
        _dispatch_kernel,
        grid_spec=grid_spec,
        out_shape=jax.ShapeDtypeStruct((n_rows, d), F32),
        compiler_params=pltpu.CompilerParams(
            dimension_semantics=("arbitrary",), vmem_limit_bytes=VMEM_LIMIT_BYTES),
        name="dispatch",
    )(pad_info, pos3, x2.reshape(t // SUBLANES, SUBLANES, d))


def _expert_kernel(texp_ref, nvalid_ref, xs_ref, gn_ref, wg_ref, wu_ref, wd_ref, ys_ref):
    i = pl.program_id(0)

    @pl.when(i < nvalid_ref[0])
    def _():
        h = _rms_norm(xs_ref[...], gn_ref[...]).astype(BF16)
        ys_ref[...] = _swiglu_rows(h, wg_ref, wu_ref, wd_ref)

    @pl.when(i >= nvalid_ref[0])
    def _():
        ys_ref[...] = jnp.zeros(ys_ref.shape, F32)


def _experts(xs, gn, wg, wu, wd, tile_expert, n_valid):
    n_rows, d = xs.shape
    tm = TM_EXP
    d_ff = wg.shape[-1]
    last = lambda i, nv: jnp.maximum(jnp.minimum(i, nv[0] - 1), 0)
    tile = lambda i, te, nv: (last(i, nv), 0)
    expert = lambda i, te, nv: (te[last(i, nv)], 0, 0)
    grid_spec = pltpu.PrefetchScalarGridSpec(
        num_scalar_prefetch=2,
        grid=(n_rows // tm,),
        in_specs=[
            pl.BlockSpec((tm, d), tile),
            pl.BlockSpec((1, d), lambda i, te, nv: (0, 0)),
            pl.BlockSpec((None, d, d_ff), expert),
            pl.BlockSpec((None, d, d_ff), expert),
            pl.BlockSpec((None, d_ff, d), expert),
        ],
        out_specs=pl.BlockSpec((tm, d), lambda i, te, nv: (i, 0)),
    )
    return pl.pallas_call(
        _expert_kernel,
        grid_spec=grid_spec,
        out_shape=jax.ShapeDtypeStruct((n_rows, d), F32),
        compiler_params=pltpu.CompilerParams(
            dimension_semantics=("arbitrary",), vmem_limit_bytes=VMEM_LIMIT_BYTES),
        name="experts",
    )(tile_expert, n_valid, xs, gn.reshape(1, -1), wg.astype(BF16), wu.astype(BF16), wd.astype(BF16))


def _combine_kernel(pos0_ref, pos_next_ref, x_ref, gates_ref, gf_ref, ys_rows, ys_hbm, o_ref, buf, sem):
    i = pl.program_id(0)
    n_groups = buf.shape[2]
    tm = n_groups * SUBLANES
    d = buf.shape[4]

    def gather(pos_ref, slot):
        def issue(q, c):
            for s in range(SUBLANES):
                for k in range(TOP_K):
                    src = pos_ref[0, 0, TOP_K * (q * SUBLANES + s) + k]
                    pltpu.make_async_copy(ys_rows.at[pl.ds(src, 1), :],
                                          buf.at[slot, k, q, pl.ds(s, 1), :], sem.at[slot, k]).start(priority=k)
            return c

        lax.fori_loop(0, n_groups, issue, 0)

    @pl.when(i == 0)
    def _():
        gather(pos0_ref, 0)

    @pl.when(i + 1 < pl.num_programs(0))
    def _():
        gather(pos_next_ref, (i + 1) % 2)

    slot = i % 2
    for k in range(TOP_K):
        pltpu.make_async_copy(ys_hbm.at[pl.ds(0, n_groups)], buf.at[slot, k], sem.at[slot, k]).wait()
    gates = gates_ref[...]
    y = (x_ref[...] + gates[:, 0:1] * buf[slot, 0].reshape(tm, d) + gates[:, 1:2] * buf[slot, 1].reshape(tm, d))
    o_ref[...] = _rms_norm(y, gf_ref[...])


def _combine(x2, gates, pos, ys, g_final):
    t, d = x2.shape
    tm = TM_DISP
    n_steps = t // tm
    pos3 = pos.reshape(n_steps, 1, TOP_K * tm)
    pos_block = (1, 1, TOP_K * tm)
    return pl.pallas_call(
        _combine_kernel,
        grid=(n_steps,),
        in_specs=[
            pl.BlockSpec(pos_block, lambda i: (0, 0, 0), memory_space=pltpu.SMEM),
            pl.BlockSpec(pos_block, lambda i: (jnp.minimum(i + 1, n_steps - 1), 0, 0), memory_space=pltpu.SMEM),
            pl.BlockSpec((tm, d), lambda i: (i, 0)),
            pl.BlockSpec((tm, LANES), lambda i: (i, 0)),
            _const_spec((1, d)),
            pl.BlockSpec(memory_space=pl.ANY),
            pl.BlockSpec(memory_space=pl.ANY),
        ],
        out_specs=pl.BlockSpec((tm, d), lambda i: (i, 0)),
        out_shape=jax.ShapeDtypeStruct(x2.shape, F32),
        scratch_shapes=[pltpu.VMEM((2, TOP_K, tm // SUBLANES, SUBLANES, d), F32),
                        pltpu.SemaphoreType.DMA((2, TOP_K))],
        compiler_params=pltpu.CompilerParams(
            dimension_semantics=("arbitrary",), vmem_limit_bytes=VMEM_LIMIT_BYTES),
        name="combine",
    )(pos3, pos3, x2, gates, g_final.reshape(1, -1), ys, ys.reshape(ys.shape[0] // SUBLANES, SUBLANES, d))


def _moe_layer(x2, gn, w_router, wg, wu, wd, g_final):
    t, _ = x2.shape
    tm = TM_EXP
    n_tiles = (TOP_K * t) // tm + N_EXPERTS
    meta, gates, counts = _router(x2, gn, w_router)

    counts = counts[0, :N_EXPERTS].astype(I32)
    padded = ((counts + tm - 1) // tm) * tm
    ends = jnp.cumsum(padded)
    starts = ends - padded
    idx = meta[0:TOP_K, :]
    rank = meta[TOP_K:2 * TOP_K, :]
    group_start = sum(jnp.where(idx == e, starts[e], 0) for e in range(N_EXPERTS))
    pos = (group_start + rank).T.reshape(-1)
    tile_start = jnp.arange(n_tiles, dtype=I32) * tm
    tile_expert = jnp.minimum(
        jnp.sum((tile_start[:, None] >= ends[None, :]).astype(I32), axis=1), N_EXPERTS - 1).astype(I32)
    n_valid = (ends[-1:] // tm).astype(I32)
    pad_info = jnp.stack([starts + counts, padded - counts, jnp.broadcast_to(n_valid, (N_EXPERTS,))]).astype(I32)

    xs = _dispatch(x2, pos, pad_info, n_tiles * tm)
    ys = _experts(xs, gn, wg, wu, wd, tile_expert, n_valid)
    return _combine(x2, gates, pos, ys, g_final)


def kernel(x, norm_mix, norm_ffn, norm_final, w_in, conv_w, lru_conv_w, lru_conv_b, lru_wa, lru_ba, lru_wx,
           lru_bx, lru_lambda, w_out, ffn_w_gate, ffn_w_up, ffn_w_down, w_router, moe_w_gate, moe_w_up,
           moe_w_down):
    bsz, seq, d = x.shape
    depth = w_in.shape[0]
    assert depth == 2, "layer 0 uses the dense FFN, layer 1 the routed FFN followed by the final norm"
    x2 = x.reshape(bsz * seq, d)

    def mixer(x2, l, w_in_l, w_out_l, sides):
        return _mixer(x2, seq, l == 0, l == depth - 1, norm_mix[l], w_in_l, conv_w[l], lru_conv_w[l],
                      lru_conv_b[l], lru_wa[l], lru_ba[l], lru_wx[l], lru_bx[l], lru_lambda[l], w_out_l, sides)

    flat = lambda w: w.reshape(-1, w.shape[-1])
    x2, (moe_gate, ffn_gate, ffn_up, ffn_down) = mixer(
        x2, 0, w_in[0].astype(BF16), w_out[0].astype(BF16),
        [flat(moe_w_gate[0]), ffn_w_gate[0], ffn_w_up[0], ffn_w_down[0]])
    x2, (moe_down, w_in_1, w_out_1) = _ffn_dense(
        x2, norm_ffn[0], ffn_gate, ffn_up, ffn_down, [flat(moe_w_down[0]), w_in[1], w_out[1]])
    x2, (moe_up,) = mixer(x2, 1, w_in_1, w_out_1, [flat(moe_w_up[0])])
    x2 = _moe_layer(x2, norm_ffn[1], w_router[0], moe_gate.reshape(moe_w_gate[0].shape),
                    moe_up.reshape(moe_w_up[0].shape), moe_down.reshape(moe_w_down[0].shape), norm_final)
    return x2.reshape(bsz, seq, d)
```

```python
import functools

import jax
import jax.numpy as jnp
from jax import lax
from jax.experimental import pallas as pl
from jax.experimental.pallas import tpu as pltpu

F32 = jnp.float32
BF16 = jnp.bfloat16
I32 = jnp.int32

EPS = 1e-6
RG_C = 8.0
W_CONV = 512
W_LRU = 512
N_EXPERTS = 8
TOP_K = 2

SUBLANES = 8
LANES = 128
MXU_DIM = 256
VMEM_LIMIT_BYTES = 56 * 1024 * 1024

TM_MIX = 512
TM_FFN = 512
FF_CHUNK = 512
TM_ROUTE = 512
TM_DISP = 512
TM_EXP = 512


def _const_spec(shape):
    nd = len(shape)
    return pl.BlockSpec(shape, lambda *_: (0,) * nd, pipeline_mode=pl.Buffered(1))


def _rms_norm(x, g):
    ms = jnp.mean(x * x, axis=-1, keepdims=True)
    return x * lax.rsqrt(ms + EPS) * g


def _sigmoid(x):
    return 1.0 / (1.0 + jnp.exp(-x))


def _gelu_tanh(x):
    c = 0.7978845608028654
    return 0.5 * x * (1.0 + jnp.tanh(c * (x + 0.044715 * (x * x * x))))


def _segment_copies(hbm, tile, buf, slot, sem, to_hbm):
    seg = buf.shape[1]
    copies = []
    for i in range(SUBLANES):
        rows = hbm.at[pl.ds(tile * (seg * SUBLANES) + i * seg, seg), :]
        vm = buf.at[slot, :, i, :]
        copies.append(pltpu.make_async_copy(vm, rows, sem.at[slot]) if to_hbm
                      else pltpu.make_async_copy(rows, vm, sem.at[slot]))
    return copies


def _segment_halo(cur, prev, n):
    first = lax.broadcasted_iota(I32, (SUBLANES, cur.shape[1]), 0) == 0
    out = []
    for k in range(n):
        rows = slice(k * SUBLANES, (k + 1) * SUBLANES)
        out.append(jnp.where(first, pltpu.roll(prev[rows, :], 1, axis=0), pltpu.roll(cur[rows, :], 1, axis=0)))
    return jnp.concatenate(out, axis=0)


def _mixer_kernel(*refs, n_side, n_tiles, tiles_per_seq, time_ordered_input, time_ordered_output):
    (x_hbm, gn_ref, win_ref, cw_ref, lcw_ref, lcb_ref, wgate_ref, ba_ref, bx_ref, lam_ref, wout_ref), refs = (
        refs[:11], refs[11:])
    side_refs, o_hbm, side_out_refs = refs[:n_side], refs[n_side], refs[n_side + 1:2 * n_side + 1]
    xbuf, p_even, p_odd, obuf, ztail, utail, hcarry, xsem, osem = refs[2 * n_side + 1:]
    k = pl.program_id(0)
    last_step = pl.num_programs(0) - 1
    seg = xbuf.shape[1]
    tm = seg * SUBLANES
    d = xbuf.shape[3]
    n_x = xbuf.shape[0]
    t_even = 2 * k
    t_odd = 2 * k + 1

    def fetch(tile):
        slot = tile % n_x
        if time_ordered_input:
            return _segment_copies(x_hbm, tile, xbuf, slot, xsem, False)
        return [pltpu.make_async_copy(x_hbm.at[pl.ds(tile * seg, seg)], xbuf.at[slot], xsem.at[slot])]

    n_o = obuf.shape[0]

    def out_slot(tile):
        return (tile + n_o) % n_o

    def write_back(tile):
        slot = out_slot(tile)
        if time_ordered_output:
            return _segment_copies(o_hbm, tile, obuf, slot, osem, True)
        return [pltpu.make_async_copy(obuf.at[slot], o_hbm.at[pl.ds(tile * seg, seg)], osem.at[slot])]

    @pl.when(k == 0)
    def _():
        for c in fetch(0) + fetch(1):
            c.start()
        xbuf[n_x - 1] = jnp.zeros(xbuf.shape[1:], F32)
        p_odd[...] = jnp.zeros(p_odd.shape, F32)
        ztail[...] = jnp.zeros(ztail.shape, F32)
        utail[...] = jnp.zeros(utail.shape, F32)
        hcarry[...] = jnp.zeros(hcarry.shape, F32)

    @pl.when(k < last_step)
    def _():
        for c in fetch(t_even) + fetch(t_odd):
            c.wait()

    @pl.when(k + 1 < last_step)
    def _():
        for c in fetch(t_even + 2) + fetch(t_odd + 2):
            c.start()

    @pl.when(k >= 3)
    def _():
        for c in write_back(t_even - 1 - n_o):
            c.wait()

    @pl.when(k >= 2)
    def _():
        for c in write_back(t_even - n_o):
            c.wait()

    def project(tile, p_ref):
        xa = xbuf[jnp.minimum(tile, n_tiles - 1) % n_x].reshape(tm, d)
        ha = _rms_norm(xa, gn_ref[...]).astype(BF16)
        p_ref[...] = jnp.dot(ha, win_ref[...], preferred_element_type=F32)

    def finish(tile, p_ref, state):
        z_tail, u_tail, h_last = state
        fresh = (tile + tiles_per_seq) % tiles_per_seq == 0
        z_tail = jnp.where(fresh, 0.0, z_tail)
        u_tail = jnp.where(fresh, 0.0, u_tail)
        h_last = jnp.where(fresh, 0.0, h_last)
        c_gate = p_ref[:, 0:W_CONV]
        b_gate = p_ref[:, W_CONV:2 * W_CONV]
        v = p_ref[:, 2 * W_CONV:3 * W_CONV]
        u = p_ref[:, 3 * W_CONV:3 * W_CONV + W_LRU]
        g = p_ref[:, 3 * W_CONV + W_LRU:]

        z = c_gate * v
        nz = z_tail.shape[0]
        z_new_tail = z[tm - nz:, :]
        zx = jnp.concatenate([_segment_halo(z_new_tail, z_tail, nz // SUBLANES), z], axis=0)
        cw = cw_ref[...]
        conv = cw[2:3, :] * z + cw[1:2, :] * zx[nz - SUBLANES:nz - SUBLANES + tm, :] + cw[0:1, :] * zx[0:tm, :]
        y_a = b_gate * conv

        nu = u_tail.shape[0]
        u_new_tail = u[tm - nu:, :]
        ux = jnp.concatenate([_segment_halo(u_new_tail, u_tail, nu // SUBLANES), u], axis=0)
        lw = lcw_ref[...]
        uc = (lw[3:4, :] * u + lw[2:3, :] * ux[2 * SUBLANES:2 * SUBLANES + tm, :]
              + lw[1:2, :] * ux[SUBLANES:SUBLANES + tm, :] + lw[0:1, :] * ux[0:tm, :] + lcb_ref[...])

        ucb = uc.astype(BF16)
        pre0 = jnp.dot(ucb[:, 0:MXU_DIM], wgate_ref[0], preferred_element_type=F32)
        pre1 = jnp.dot(ucb[:, MXU_DIM:], wgate_ref[1], preferred_element_type=F32)
        pre_a = jnp.concatenate([pre0[:, :MXU_DIM], pre1[:, :MXU_DIM]], axis=1) + ba_ref[...]
        pre_x = jnp.concatenate([pre0[:, MXU_DIM:], pre1[:, MXU_DIM:]], axis=1) + bx_ref[...]
        r = _sigmoid(pre_a)
        i_gate = _sigmoid(pre_x)
        neg_lam = -lam_ref[...]
        softplus = jnp.maximum(neg_lam, 0.0) + jnp.log1p(jnp.exp(-jnp.abs(neg_lam)))
        log_a = (-RG_C) * r * softplus
        a = jnp.exp(log_a)
        th = jnp.tanh(log_a)
        b = jnp.sqrt((-2.0 * th) / (1.0 - th)) * (i_gate * uc)

        a3 = a.reshape(seg, SUBLANES, W_LRU)
        b3 = b.reshape(seg, SUBLANES, W_LRU)
        prod = [a3[0]]
        part = [b3[0]]
        for j in range(1, seg):
            prod.append(a3[j] * prod[j - 1])
            part.append(a3[j] * part[j - 1] + b3[j])
        init_rows = []
        for i in range(SUBLANES):
            init_rows.append(h_last)
            h_last = prod[seg - 1][i:i + 1, :] * h_last + part[seg - 1][i:i + 1, :]
        init = jnp.concatenate(init_rows, axis=0)
        hs = jnp.stack([part[j] + prod[j] * init for j in range(seg)], axis=0).reshape(tm, W_LRU)

        y_b = hs * _gelu_tanh(g)
        y = jnp.concatenate([y_a, y_b], axis=1).astype(BF16)
        xc = xbuf[(tile + n_x) % n_x].reshape(tm, d)
        out = xc + jnp.dot(y, wout_ref[...], preferred_element_type=F32)
        return out.reshape(seg, SUBLANES, d), (z_new_tail, u_new_tail, h_last)

    for side_ref, side_out_ref in zip(side_refs, side_out_refs):
        side_out_ref[...] = side_ref[...].astype(BF16)
    state = (ztail[...], utail[...], hcarry[0:1, :])
    project(t_even, p_even)
    obuf[out_slot(t_even - 1)], state = finish(t_even - 1, p_odd, state)
    project(t_odd, p_odd)
    obuf[out_slot(t_even)], state = finish(t_even, p_even, state)
    ztail[...], utail[...] = state[0], state[1]
    hcarry[...] = jnp.broadcast_to(state[2], hcarry.shape)

    @pl.when(k >= 1)
    def _():
        for c in write_back(t_even - 1):
            c.start()

    @pl.when(k < last_step)
    def _():
        for c in write_back(t_even):
            c.start()

    @pl.when(k == last_step)
    def _():
        for c in write_back(t_even - 3) + write_back(t_even - 2) + write_back(t_even - 1):
            c.wait()


def _block_diag_halves(w):
    nh, dh, _ = w.shape
    per = MXU_DIM // dh
    w4 = w.reshape(nh // per, per, dh, dh)
    eye = jnp.eye(per, dtype=w.dtype)
    return jnp.einsum("kaij,ab->kaibj", w4, eye).reshape(nh // per, MXU_DIM, MXU_DIM)


def _side_cast_specs(sides, n_blocks):
    index = lambda i: (jnp.minimum(i, n_blocks - 1), 0)
    specs = [pl.BlockSpec((s.shape[0] // n_blocks, s.shape[1]), index) for s in sides]
    return specs, list(specs), [jax.ShapeDtypeStruct(s.shape, BF16) for s in sides]


def _mixer(x2, seq, time_ordered_input, time_ordered_output, gn, w_in, conv_w, lru_conv_w, lru_conv_b, wa, ba,
           wx, bx, lam, w_out, sides):
    t, d = x2.shape
    d_in = w_in.shape[1]
    tm = TM_MIX
    seg = tm // SUBLANES
    n_tiles = t // tm
    assert n_tiles % 2 == 0 and (seq // tm) % 2 == 0
    wgate = jnp.concatenate([_block_diag_halves(wa), _block_diag_halves(wx)], axis=-1).astype(BF16)
    row = lambda a: a.reshape(1, -1)
    x_in = x2 if time_ordered_input else x2.reshape(t // SUBLANES, SUBLANES, d)
    side_in, side_out, side_shapes = _side_cast_specs(sides, n_tiles // 2)
    out, *sides_bf16 = pl.pallas_call(
        functools.partial(_mixer_kernel, n_side=len(sides), n_tiles=n_tiles, tiles_per_seq=seq // tm,
                          time_ordered_input=time_ordered_input, time_ordered_output=time_ordered_output),
        grid=(n_tiles // 2 + 1,),
        in_specs=[
            pl.BlockSpec(memory_space=pl.ANY),
            _const_spec((1, d)),
            _const_spec((d, d_in)),
            _const_spec(conv_w.shape),
            _const_spec(lru_conv_w.shape),
            _const_spec((1, W_LRU)),
            _const_spec(wgate.shape),
            _const_spec((1, W_LRU)),
            _const_spec((1, W_LRU)),
            _const_spec((1, W_LRU)),
            _const_spec(w_out.shape),
            *side_in,
        ],
        out_specs=[pl.BlockSpec(memory_space=pl.ANY), *side_out],
        out_shape=[jax.ShapeDtypeStruct((t, d) if time_ordered_output else (t // SUBLANES, SUBLANES, d), F32),
                   *side_shapes],
        scratch_shapes=[
            pltpu.VMEM((6, seg, SUBLANES, d), F32),
            pltpu.VMEM((tm, d_in), F32),
            pltpu.VMEM((tm, d_in), F32),
            pltpu.VMEM((4, seg, SUBLANES, d), F32),
            pltpu.VMEM(((conv_w.shape[0] - 1) * SUBLANES, W_CONV), F32),
            pltpu.VMEM(((lru_conv_w.shape[0] - 1) * SUBLANES, W_LRU), F32),
            pltpu.VMEM((SUBLANES, W_LRU), F32),
            pltpu.SemaphoreType.DMA((6,)),
            pltpu.SemaphoreType.DMA((4,)),
        ],
        compiler_params=pltpu.CompilerParams(
            dimension_semantics=("arbitrary",), vmem_limit_bytes=VMEM_LIMIT_BYTES),
        name="mixer",
    )(x_in, row(gn), w_in, conv_w, lru_conv_w, row(lru_conv_b), wgate, row(ba), row(bx), row(lam), w_out, *sides)
    return out.reshape(t, d), sides_bf16


def _swiglu_rows(h, wg_ref, wu_ref, wd_ref):
    d_ff = wg_ref.shape[-1]
    acc = jnp.zeros((h.shape[0], wd_ref.shape[-1]), F32)
    for c in range(d_ff // FF_CHUNK):
        cols = slice(c * FF_CHUNK, (c + 1) * FF_CHUNK)
        gate = jnp.dot(h, wg_ref[:, cols], preferred_element_type=F32)
        up = jnp.dot(h, wu_ref[:, cols], preferred_element_type=F32)
        act = (gate * _sigmoid(gate) * up).astype(BF16)
        acc = acc + jnp.dot(act, wd_ref[cols, :], preferred_element_type=F32)
    return acc


def _ffn_kernel(*refs, n_side):
    x_ref, gn_ref, wg_ref, wu_ref, wd_ref = refs[:5]
    side_refs, o_ref, side_out_refs = refs[5:5 + n_side], refs[5 + n_side], refs[6 + n_side:]
    for side_ref, side_out_ref in zip(side_refs, side_out_refs):
        side_out_ref[...] = side_ref[...].astype(BF16)
    x = x_ref[...]
    h = _rms_norm(x, gn_ref[...]).astype(BF16)
    o_ref[...] = x + _swiglu_rows(h, wg_ref, wu_ref, wd_ref)


def _ffn_dense(x2, gn, wg, wu, wd, sides):
    t, d = x2.shape
    tm = TM_FFN
    side_in, side_out, side_shapes = _side_cast_specs(sides, t // tm)
    out, *sides_bf16 = pl.pallas_call(
        functools.partial(_ffn_kernel, n_side=len(sides)),
        grid=(t // tm,),
        in_specs=[
            pl.BlockSpec((tm, d), lambda i: (i, 0)),
            _const_spec((1, d)),
            _const_spec(wg.shape),
            _const_spec(wu.shape),
            _const_spec(wd.shape),
            *side_in,
        ],
        out_specs=[pl.BlockSpec((tm, d), lambda i: (i, 0)), *side_out],
        out_shape=[jax.ShapeDtypeStruct(x2.shape, F32), *side_shapes],
        compiler_params=pltpu.CompilerParams(
            dimension_semantics=("arbitrary",), vmem_limit_bytes=VMEM_LIMIT_BYTES),
        name="ffn_dense",
    )(x2, gn.reshape(1, -1), wg, wu, wd, *sides)
    return out, sides_bf16


def _split_bf16(a):
    hi = a.astype(BF16)
    return hi, (a - hi.astype(F32)).astype(BF16)


def _router_kernel(*refs, n_side):
    x_ref, gn_ref, wr_hi_ref, wr_lo_ref = refs[:4]
    side_refs, (meta_ref, gates_ref, counts_ref) = refs[4:4 + n_side], refs[4 + n_side:7 + n_side]
    side_out_refs, run = refs[7 + n_side:7 + 2 * n_side], refs[7 + 2 * n_side]
    i = pl.program_id(0)
    tm = x_ref.shape[0]
    for side_ref, side_out_ref in zip(side_refs, side_out_refs):
        side_out_ref[...] = side_ref[...].astype(BF16)

    @pl.when(i == 0)
    def _():
        run[...] = jnp.zeros((1, LANES), F32)

    h_hi, h_lo = _split_bf16(_rms_norm(x_ref[...], gn_ref[...]))
    logits = (jnp.dot(h_hi, wr_hi_ref[...], preferred_element_type=F32)
              + jnp.dot(h_lo, wr_hi_ref[...], preferred_element_type=F32)
              + jnp.dot(h_hi, wr_lo_ref[...], preferred_element_type=F32))
    lane = lax.broadcasted_iota(I32, (tm, LANES), 1)
    neg_inf = jnp.float32(-jnp.inf)
    logits = jnp.where(lane < N_EXPERTS, logits, neg_inf)
    m1 = jnp.max(logits, axis=-1, keepdims=True)
    i1 = jnp.min(jnp.where(logits == m1, lane, LANES), axis=-1, keepdims=True)
    rest = jnp.where(lane == i1, neg_inf, logits)
    m2 = jnp.max(rest, axis=-1, keepdims=True)
    i2 = jnp.min(jnp.where(rest == m2, lane, LANES), axis=-1, keepdims=True)
    e = jnp.exp(m2 - m1)
    g1 = 1.0 / (1.0 + e)
    g2 = e * g1

    sel1 = lane == i1
    sel2 = lane == i2
    hot = jnp.where(sel1 | sel2, 1.0, 0.0)
    rows = lax.broadcasted_iota(I32, (tm, tm), 0)
    cols = lax.broadcasted_iota(I32, (tm, tm), 1)
    strict_lower = jnp.where(cols < rows, 1.0, 0.0).astype(BF16)
    before = jnp.dot(strict_lower, hot.astype(BF16), preferred_element_type=F32) + run[...]
    rank1 = jnp.sum(jnp.where(sel1, before, 0.0), axis=-1, keepdims=True)
    rank2 = jnp.sum(jnp.where(sel2, before, 0.0), axis=-1, keepdims=True)
    run[...] = run[...] + jnp.sum(hot, axis=0, keepdims=True)

    meta = jnp.where(lane == 0, i1, jnp.where(lane == 1, i2, jnp.where(
        lane == 2, rank1.astype(I32), jnp.where(lane == 3, rank2.astype(I32), 0))))
    meta_ref[...] = jnp.transpose(meta)[0:SUBLANES, :]
    gates_ref[...] = jnp.where(lane == 0, g1, jnp.where(lane == 1, g2, 0.0))
    counts_ref[...] = jnp.broadcast_to(run[...], (SUBLANES, LANES))


def _router(x2, gn, w_router, sides):
    t, d = x2.shape
    tm = TM_ROUTE
    wr_hi, wr_lo = _split_bf16(jnp.zeros((d, LANES), F32).at[:, :N_EXPERTS].set(w_router))
    side_in, side_out, side_shapes = _side_cast_specs(sides, t // tm)
    meta, gates, counts, *sides_bf16 = pl.pallas_call(
        functools.partial(_router_kernel, n_side=len(sides)),
        grid=(t // tm,),
        in_specs=[
            pl.BlockSpec((tm, d), lambda i: (i, 0)),
            _const_spec((1, d)),
            _const_spec((d, LANES)),
            _const_spec((d, LANES)),
            *side_in,
        ],
        out_specs=[
            pl.BlockSpec((SUBLANES, tm), lambda i: (0, i)),
            pl.BlockSpec((tm, LANES), lambda i: (i, 0)),
            pl.BlockSpec((SUBLANES, LANES), lambda i: (0, 0)),
            *side_out,
        ],
        out_shape=[
            jax.ShapeDtypeStruct((SUBLANES, t), I32),
            jax.ShapeDtypeStruct((t, LANES), F32),
            jax.ShapeDtypeStruct((SUBLANES, LANES), F32),
            *side_shapes,
        ],
        scratch_shapes=[pltpu.VMEM((1, LANES), F32)],
        compiler_params=pltpu.CompilerParams(
            dimension_semantics=("arbitrary",), vmem_limit_bytes=VMEM_LIMIT_BYTES),
        name="router",
    )(x2, gn.reshape(1, -1), wr_hi, wr_lo, *sides)
    return meta, gates, counts, sides_bf16


def _row_copy(src_ref, src_row, dst_ref, dst_row, sem):
    return pltpu.make_async_copy(src_ref.at[pl.ds(src_row, 1), :], dst_ref.at[pl.ds(dst_row, 1), :], sem)


def _dispatch_kernel(pad_ref, pos_ref, x_ref, xs_hbm, ztile, sem, zsem):
    i = pl.program_id(0)
    tile_rows = ztile.shape[0]

    @pl.when(i == 0)
    def _():
        ztile[...] = jnp.zeros(ztile.shape, F32)
        for e in range(N_EXPERTS):
            start = pad_ref[0, e]
            n = pad_ref[1, e]

            def fill(q, c):
                _row_copy(ztile, 0, xs_hbm, start + q, zsem).start()
                return c

            def drain(q, c):
                _row_copy(ztile, 0, xs_hbm, start, zsem).wait()
                return c

            lax.fori_loop(0, n, fill, 0)
            lax.fori_loop(0, n, drain, 0)

        def tail_copy(q):
            r0 = pl.multiple_of(q * tile_rows, tile_rows)
            return pltpu.make_async_copy(ztile, xs_hbm.at[pl.ds(r0, tile_rows), :], zsem)

        def fill_tail(q, c):
            tail_copy(q).start()
            return c

        def drain_tail(q, c):
            tail_copy(q).wait()
            return c

        n_valid = pad_ref[2, 0]
        n_tiles = xs_hbm.shape[0] // tile_rows
        lax.fori_loop(n_valid, n_tiles, fill_tail, 0)
        lax.fori_loop(n_valid, n_tiles, drain_tail, 0)

    def issue(q, c):
        for s in range(SUBLANES):
            for k in range(TOP_K):
                dst = pos_ref[0, 0, TOP_K * (q * SUBLANES + s) + k]
                pltpu.make_async_copy(x_ref.at[q, pl.ds(s, 1), :], xs_hbm.at[pl.ds(dst, 1), :],
                                      sem.at[k]).start(priority=k)
        return c

    n_groups = x_ref.shape[0]
    lax.fori_loop(0, n_groups, issue, 0)
    for k in range(TOP_K):
        for _ in range(n_groups * SUBLANES // tile_rows):
            pltpu.make_async_copy(ztile, xs_hbm.at[pl.ds(0, tile_rows), :], sem.at[k]).wait()


def _dispatch(x2, pos, pad_info, n_rows):
    t, d = x2.shape
    tm = TM_DISP
    pos3 = pos.reshape(t // tm, 1, TOP_K * tm)
    grid_spec = pltpu.PrefetchScalarGridSpec(
        num_scalar_prefetch=1,
        grid=(t // tm,),
        in_specs=[
            pl.BlockSpec((1, 1, TOP_K * tm), lambda i, pad: (i, 0, 0), memory_space=pltpu.SMEM),
            pl.BlockSpec((tm // SUBLANES, SUBLANES, d), lambda i, pad: (i, 0, 0)),
        ],
        out_specs=pl.BlockSpec(memory_space=pl.ANY),
        scratch_shapes=[
            pltpu.VMEM((TM_EXP, d), F32),
            pltpu.SemaphoreType.DMA((TOP_K,)),
            pltpu.SemaphoreType.DMA(()),
        ],
    )
    return pl.pallas_call(
        _dispatch_kernel,
        grid_spec=grid_spec,
        out_shape=jax.ShapeDtypeStruct((n_rows, d), F32),
        compiler_params=pltpu.CompilerParams(
            dimension_semantics=("arbitrary",), vmem_limit_bytes=VMEM_LIMIT_BYTES),
        name="dispatch",
    )(pad_info, pos3, x2.reshape(t // SUBLANES, SUBLANES, d))


def _expert_kernel(texp_ref, nvalid_ref, xs_ref, gn_ref, wg_ref, wu_ref, wd_ref, ys_ref):
    i = pl.program_id(0)

    @pl.when(i < nvalid_ref[0])
    def _():
        h = _rms_norm(xs_ref[...], gn_ref[...]).astype(BF16)
        ys_ref[...] = _swiglu_rows(h, wg_ref, wu_ref, wd_ref)

    @pl.when(i >= nvalid_ref[0])
    def _():
        ys_ref[...] = jnp.zeros(ys_ref.shape, F32)


def _experts(xs, gn, wg, wu, wd, tile_expert, n_valid):
    n_rows, d = xs.shape
    tm = TM_EXP
    d_ff = wg.shape[-1]
    last = lambda i, nv: jnp.maximum(jnp.minimum(i, nv[0] - 1), 0)
    tile = lambda i, te, nv: (last(i, nv), 0)
    expert = lambda i, te, nv: (te[last(i, nv)], 0, 0)
    grid_spec = pltpu.PrefetchScalarGridSpec(
        num_scalar_prefetch=2,
        grid=(n_rows // tm,),
        in_specs=[
            pl.BlockSpec((tm, d), tile),
            pl.BlockSpec((1, d), lambda i, te, nv: (0, 0)),
            pl.BlockSpec((None, d, d_ff), expert),
            pl.BlockSpec((None, d, d_ff), expert),
            pl.BlockSpec((None, d_ff, d), expert),
        ],
        out_specs=pl.BlockSpec((tm, d), lambda i, te, nv: (i, 0)),
    )
    return pl.pallas_call(
        _expert_kernel,
        grid_spec=grid_spec,
        out_shape=jax.ShapeDtypeStruct((n_rows, d), F32),
        compiler_params=pltpu.CompilerParams(
            dimension_semantics=("arbitrary",), vmem_limit_bytes=VMEM_LIMIT_BYTES),
        name="experts",
    )(tile_expert, n_valid, xs, gn.reshape(1, -1), wg.astype(BF16), wu.astype(BF16), wd.astype(BF16))


def _combine_kernel(pos0_ref, pos_next_ref, x_ref, gates_ref, gf_ref, ys_rows, ys_hbm, o_ref, buf, sem):
    i = pl.program_id(0)
    n_groups = buf.shape[2]
    tm = n_groups * SUBLANES
    d = buf.shape[4]

    def gather(pos_ref, slot):
        def issue(q, c):
            for s in range(SUBLANES):
                for k in range(TOP_K):
                    src = pos_ref[0, 0, TOP_K * (q * SUBLANES + s) + k]
                    pltpu.make_async_copy(ys_rows.at[pl.ds(src, 1), :],
                                          buf.at[slot, k, q, pl.ds(s, 1), :], sem.at[slot, k]).start(priority=k)
            return c

        lax.fori_loop(0, n_groups, issue, 0)

    @pl.when(i == 0)
    def _():
        gather(pos0_ref, 0)

    @pl.when(i + 1 < pl.num_programs(0))
    def _():
        gather(pos_next_ref, (i + 1) % 2)

    slot = i % 2
    for k in range(TOP_K):
        pltpu.make_async_copy(ys_hbm.at[pl.ds(0, n_groups)], buf.at[slot, k], sem.at[slot, k]).wait()
    gates = gates_ref[...]
    y = (x_ref[...] + gates[:, 0:1] * buf[slot, 0].reshape(tm, d) + gates[:, 1:2] * buf[slot, 1].reshape(tm, d))
    o_ref[...] = _rms_norm(y, gf_ref[...])


def _combine(x2, gates, pos, ys, g_final):
    t, d = x2.shape
    tm = TM_DISP
    n_steps = t // tm
    pos3 = pos.reshape(n_steps, 1, TOP_K * tm)
    pos_block = (1, 1, TOP_K * tm)
    return pl.pallas_call(
        _combine_kernel,
        grid=(n_steps,),
        in_specs=[
            pl.BlockSpec(pos_block, lambda i: (0, 0, 0), memory_space=pltpu.SMEM),
            pl.BlockSpec(pos_block, lambda i: (jnp.minimum(i + 1, n_steps - 1), 0, 0), memory_space=pltpu.SMEM),
            pl.BlockSpec((tm, d), lambda i: (i, 0)),
            pl.BlockSpec((tm, LANES), lambda i: (i, 0)),
            _const_spec((1, d)),
            pl.BlockSpec(memory_space=pl.ANY),
            pl.BlockSpec(memory_space=pl.ANY),
        ],
        out_specs=pl.BlockSpec((tm, d), lambda i: (i, 0)),
        out_shape=jax.ShapeDtypeStruct(x2.shape, F32),
        scratch_shapes=[pltpu.VMEM((2, TOP_K, tm // SUBLANES, SUBLANES, d), F32),
                        pltpu.SemaphoreType.DMA((2, TOP_K))],
        compiler_params=pltpu.CompilerParams(
            dimension_semantics=("arbitrary",), vmem_limit_bytes=VMEM_LIMIT_BYTES),
        name="combine",
    )(pos3, pos3, x2, gates, g_final.reshape(1, -1), ys, ys.reshape(ys.shape[0] // SUBLANES, SUBLANES, d))


def _moe_layer(x2, gn, w_router, wg, wu_f32, wd, g_final):
    t, _ = x2.shape
    tm = TM_EXP
    n_tiles = (TOP_K * t) // tm + N_EXPERTS
    meta, gates, counts, (wu,) = _router(x2, gn, w_router, [wu_f32.reshape(-1, wu_f32.shape[-1])])
    wu = wu.reshape(wu_f32.shape)

    counts = counts[0, :N_EXPERTS].astype(I32)
    padded = ((counts + tm - 1) // tm) * tm
    ends = jnp.cumsum(padded)
    starts = ends - padded
    idx = meta[0:TOP_K, :]
    rank = meta[TOP_K:2 * TOP_K, :]
    group_start = sum(jnp.where(idx == e, starts[e], 0) for e in range(N_EXPERTS))
    pos = (group_start + rank).T.reshape(-1)
    tile_start = jnp.arange(n_tiles, dtype=I32) * tm
    tile_expert = jnp.minimum(
        jnp.sum((tile_start[:, None] >= ends[None, :]).astype(I32), axis=1), N_EXPERTS - 1).astype(I32)
    n_valid = (ends[-1:] // tm).astype(I32)
    pad_info = jnp.stack([starts + counts, padded - counts, jnp.broadcast_to(n_valid, (N_EXPERTS,))]).astype(I32)

    xs = _dispatch(x2, pos, pad_info, n_tiles * tm)
    ys = _experts(xs, gn, wg, wu, wd, tile_expert, n_valid)
    return _combine(x2, gates, pos, ys, g_final)


def kernel(x, norm_mix, norm_ffn, norm_final, w_in, conv_w, lru_conv_w, lru_conv_b, lru_wa, lru_ba, lru_wx,
           lru_bx, lru_lambda, w_out, ffn_w_gate, ffn_w_up, ffn_w_down, w_router, moe_w_gate, moe_w_up,
           moe_w_down):
    bsz, seq, d = x.shape
    depth = w_in.shape[0]
    assert depth == 2, "layer 0 uses the dense FFN, layer 1 the routed FFN followed by the final norm"
    x2 = x.reshape(bsz * seq, d)

    def mixer(x2, l, w_in_l, w_out_l, sides):
        return _mixer(x2, seq, l == 0, l == depth - 1, norm_mix[l], w_in_l, conv_w[l], lru_conv_w[l],
                      lru_conv_b[l], lru_wa[l], lru_ba[l], lru_wx[l], lru_bx[l], lru_lambda[l], w_out_l, sides)

    flat = lambda w: w.reshape(-1, w.shape[-1])
    x2, (ffn_gate, ffn_up, ffn_down) = mixer(
        x2, 0, w_in[0].astype(BF16), w_out[0].astype(BF16), [ffn_w_gate[0], ffn_w_up[0], ffn_w_down[0]])
    x2, (moe_gate, moe_down, w_in_1, w_out_1) = _ffn_dense(
        x2, norm_ffn[0], ffn_gate, ffn_up, ffn_down,
        [flat(moe_w_gate[0]), flat(moe_w_down[0]), w_in[1], w_out[1]])
    x2, _ = mixer(x2, 1, w_in_1, w_out_1, [])
    x2 = _moe_layer(x2, norm_ffn[1], w_router[0], moe_gate.reshape(moe_w_gate[0].shape), moe_w_up[0],
                    moe_down.reshape(moe_w_down[0].shape), norm_final)
    return x2.reshape(bsz, seq, d)
```

```python
import functools

import jax
import jax.numpy as jnp
from jax import lax
from jax.experimental import pallas as pl
from jax.experimental.pallas import tpu as pltpu

F32 = jnp.float32
BF16 = jnp.bfloat16
I32 = jnp.int32

EPS = 1e-6
RG_C = 8.0
W_CONV = 512
W_LRU = 512
N_EXPERTS = 8
TOP_K = 2

SUBLANES = 8
LANES = 128
MXU_DIM = 256
VMEM_LIMIT_BYTES = 56 * 1024 * 1024

TM_MIX = 512
TM_FFN = 512
FF_CHUNK = 512
TM_ROUTE = 512
TM_DISP = 512
TM_EXP = 512


def _const_spec(shape):
    nd = len(shape)
    return pl.BlockSpec(shape, lambda *_: (0,) * nd, pipeline_mode=pl.Buffered(1))


def _rms_norm(x, g):
    ms = jnp.mean(x * x, axis=-1, keepdims=True)
    return x * lax.rsqrt(ms + EPS) * g


def _sigmoid(x):
    return 1.0 / (1.0 + jnp.exp(-x))


def _sigmoid_via_tanh(x):
    return 0.5 * jnp.tanh(0.5 * x) + 0.5


def _gelu_tanh(x):
    c = 0.7978845608028654
    return 0.5 * x * (1.0 + jnp.tanh(c * (x + 0.044715 * (x * x * x))))


def _segment_copies(hbm, tile, buf, slot, sem, to_hbm):
    seg = buf.shape[1]
    copies = []
    for i in range(SUBLANES):
        rows = hbm.at[pl.ds(tile * (seg * SUBLANES) + i * seg, seg), :]
        vm = buf.at[slot, :, i, :]
        copies.append(pltpu.make_async_copy(vm, rows, sem.at[slot]) if to_hbm
                      else pltpu.make_async_copy(rows, vm, sem.at[slot]))
    return copies


def _segment_halo(cur, prev, n):
    first = lax.broadcasted_iota(I32, (SUBLANES, cur.shape[1]), 0) == 0
    out = []
    for k in range(n):
        rows = slice(k * SUBLANES, (k + 1) * SUBLANES)
        out.append(jnp.where(first, pltpu.roll(prev[rows, :], 1, axis=0), pltpu.roll(cur[rows, :], 1, axis=0)))
    return jnp.concatenate(out, axis=0)


def _mixer_kernel(*refs, n_side, n_tiles, tiles_per_seq, time_ordered_input, time_ordered_output):
    (x_hbm, gn_ref, win_ref, cw_ref, lcw_ref, lcb_ref, wgate_ref, ba_ref, bx_ref, lam_ref, wout_ref), refs = (
        refs[:11], refs[11:])
    side_refs, o_hbm, side_out_refs = refs[:n_side], refs[n_side], refs[n_side + 1:2 * n_side + 1]
    xbuf, p_even, p_odd, obuf, ztail, utail, hcarry, xsem, osem = refs[2 * n_side + 1:]
    k = pl.program_id(0)
    last_step = pl.num_programs(0) - 1
    seg = xbuf.shape[1]
    tm = seg * SUBLANES
    d = xbuf.shape[3]
    n_x = xbuf.shape[0]
    t_even = 2 * k
    t_odd = 2 * k + 1

    def fetch(tile):
        slot = tile % n_x
        if time_ordered_input:
            return _segment_copies(x_hbm, tile, xbuf, slot, xsem, False)
        return [pltpu.make_async_copy(x_hbm.at[pl.ds(tile * seg, seg)], xbuf.at[slot], xsem.at[slot])]

    n_o = obuf.shape[0]

    def out_slot(tile):
        return (tile + n_o) % n_o

    def write_back(tile):
        slot = out_slot(tile)
        if time_ordered_output:
            return _segment_copies(o_hbm, tile, obuf, slot, osem, True)
        return [pltpu.make_async_copy(obuf.at[slot], o_hbm.at[pl.ds(tile * seg, seg)], osem.at[slot])]

    @pl.when(k == 0)
    def _():
        for c in fetch(0) + fetch(1):
            c.start()
        xbuf[n_x - 1] = jnp.zeros(xbuf.shape[1:], F32)
        p_odd[...] = jnp.zeros(p_odd.shape, F32)
        ztail[...] = jnp.zeros(ztail.shape, F32)
        utail[...] = jnp.zeros(utail.shape, F32)
        hcarry[...] = jnp.zeros(hcarry.shape, F32)

    @pl.when(k < last_step)
    def _():
        for c in fetch(t_even) + fetch(t_odd):
            c.wait()

    @pl.when(k + 1 < last_step)
    def _():
        for c in fetch(t_even + 2) + fetch(t_odd + 2):
            c.start()

    @pl.when(k >= 3)
    def _():
        for c in write_back(t_even - 1 - n_o):
            c.wait()

    @pl.when(k >= 2)
    def _():
        for c in write_back(t_even - n_o):
            c.wait()

    def project(tile, p_ref):
        xa = xbuf[jnp.minimum(tile, n_tiles - 1) % n_x].reshape(tm, d)
        ha = _rms_norm(xa, gn_ref[...]).astype(BF16)
        p_ref[...] = jnp.dot(ha, win_ref[...], preferred_element_type=F32)

    def finish(tile, p_ref, state):
        z_tail, u_tail, h_last = state
        fresh = (tile + tiles_per_seq) % tiles_per_seq == 0
        z_tail = jnp.where(fresh, 0.0, z_tail)
        u_tail = jnp.where(fresh, 0.0, u_tail)
        h_last = jnp.where(fresh, 0.0, h_last)
        c_gate = p_ref[:, 0:W_CONV]
        b_gate = p_ref[:, W_CONV:2 * W_CONV]
        v = p_ref[:, 2 * W_CONV:3 * W_CONV]
        u = p_ref[:, 3 * W_CONV:3 * W_CONV + W_LRU]
        g = p_ref[:, 3 * W_CONV + W_LRU:]

        z = c_gate * v
        nz = z_tail.shape[0]
        z_new_tail = z[tm - nz:, :]
        zx = jnp.concatenate([_segment_halo(z_new_tail, z_tail, nz // SUBLANES), z], axis=0)
        cw = cw_ref[...]
        conv = cw[2:3, :] * z + cw[1:2, :] * zx[nz - SUBLANES:nz - SUBLANES + tm, :] + cw[0:1, :] * zx[0:tm, :]
        y_a = b_gate * conv

        nu = u_tail.shape[0]
        u_new_tail = u[tm - nu:, :]
        ux = jnp.concatenate([_segment_halo(u_new_tail, u_tail, nu // SUBLANES), u], axis=0)
        lw = lcw_ref[...]
        uc = (lw[3:4, :] * u + lw[2:3, :] * ux[2 * SUBLANES:2 * SUBLANES + tm, :]
              + lw[1:2, :] * ux[SUBLANES:SUBLANES + tm, :] + lw[0:1, :] * ux[0:tm, :] + lcb_ref[...])

        ucb = uc.astype(BF16)
        pre0 = jnp.dot(ucb[:, 0:MXU_DIM], wgate_ref[0], preferred_element_type=F32)
        pre1 = jnp.dot(ucb[:, MXU_DIM:], wgate_ref[1], preferred_element_type=F32)
        pre_a = jnp.concatenate([pre0[:, :MXU_DIM], pre1[:, :MXU_DIM]], axis=1) + ba_ref[...]
        pre_x = jnp.concatenate([pre0[:, MXU_DIM:], pre1[:, MXU_DIM:]], axis=1) + bx_ref[...]
        r = _sigmoid_via_tanh(pre_a)
        i_gate = _sigmoid_via_tanh(pre_x)
        neg_lam = -lam_ref[...]
        softplus = jnp.maximum(neg_lam, 0.0) + jnp.log1p(jnp.exp(-jnp.abs(neg_lam)))
        log_a = (-RG_C) * r * softplus
        a = jnp.exp(log_a)
        th = jnp.tanh(log_a)
        b = jnp.sqrt((-2.0 * th) / (1.0 - th)) * (i_gate * uc)

        a3 = a.reshape(seg, SUBLANES, W_LRU)
        b3 = b.reshape(seg, SUBLANES, W_LRU)
        prod = [a3[0]]
        part = [b3[0]]
        for j in range(1, seg):
            prod.append(a3[j] * prod[j - 1])
            part.append(a3[j] * part[j - 1] + b3[j])
        init_rows = []
        for i in range(SUBLANES):
            init_rows.append(h_last)
            h_last = prod[seg - 1][i:i + 1, :] * h_last + part[seg - 1][i:i + 1, :]
        init = jnp.concatenate(init_rows, axis=0)
        hs = jnp.stack([part[j] + prod[j] * init for j in range(seg)], axis=0).reshape(tm, W_LRU)

        y_b = hs * _gelu_tanh(g)
        y = jnp.concatenate([y_a, y_b], axis=1).astype(BF16)
        xc = xbuf[(tile + n_x) % n_x].reshape(tm, d)
        out = xc + jnp.dot(y, wout_ref[...], preferred_element_type=F32)
        return out.reshape(seg, SUBLANES, d), (z_new_tail, u_new_tail, h_last)

    for side_ref, side_out_ref in zip(side_refs, side_out_refs):
        side_out_ref[...] = side_ref[...].astype(BF16)
    state = (ztail[...], utail[...], hcarry[0:1, :])
    project(t_even, p_even)
    obuf[out_slot(t_even - 1)], state = finish(t_even - 1, p_odd, state)
    project(t_odd, p_odd)
    obuf[out_slot(t_even)], state = finish(t_even, p_even, state)
    ztail[...], utail[...] = state[0], state[1]
    hcarry[...] = jnp.broadcast_to(state[2], hcarry.shape)

    @pl.when(k >= 1)
    def _():
        for c in write_back(t_even - 1):
            c.start()

    @pl.when(k < last_step)
    def _():
        for c in write_back(t_even):
            c.start()

    @pl.when(k == last_step)
    def _():
        for c in write_back(t_even - 3) + write_back(t_even - 2) + write_back(t_even - 1):
            c.wait()


def _block_diag_halves(w):
    nh, dh, _ = w.shape
    per = MXU_DIM // dh
    w4 = w.reshape(nh // per, per, dh, dh)
    eye = jnp.eye(per, dtype=w.dtype)
    return jnp.einsum("kaij,ab->kaibj", w4, eye).reshape(nh // per, MXU_DIM, MXU_DIM)


def _side_cast_specs(sides, n_blocks):
    row = lambda i: jnp.minimum(i, n_blocks - 1)
    in_specs, out_specs, shapes, arrays = [], [], [], []
    for side in sides:
        arr, layer = side if isinstance(side, tuple) else (side, None)
        rows, cols = arr.shape[-2:]
        block = (rows // n_blocks, cols)
        if layer is None:
            in_specs.append(pl.BlockSpec(block, lambda i: (row(i), 0)))
        else:
            in_specs.append(pl.BlockSpec((None,) + block, lambda i, layer=layer: (layer, row(i), 0)))
        out_specs.append(pl.BlockSpec(block, lambda i: (row(i), 0)))
        shapes.append(jax.ShapeDtypeStruct((rows, cols), BF16))
        arrays.append(arr)
    return in_specs, out_specs, shapes, arrays


def _mixer(x2, seq, time_ordered_input, time_ordered_output, gn, w_in, conv_w, lru_conv_w, lru_conv_b, wa, ba,
           wx, bx, lam, w_out, sides):
    t, d = x2.shape
    d_in = w_in.shape[1]
    tm = TM_MIX
    seg = tm // SUBLANES
    n_tiles = t // tm
    assert n_tiles % 2 == 0 and (seq // tm) % 2 == 0
    wgate = jnp.concatenate([_block_diag_halves(wa), _block_diag_halves(wx)], axis=-1).astype(BF16)
    row = lambda a: a.reshape(1, -1)
    x_in = x2 if time_ordered_input else x2.reshape(t // SUBLANES, SUBLANES, d)
    side_in, side_out, side_shapes, side_arrays = _side_cast_specs(sides, n_tiles // 2)
    out, *sides_bf16 = pl.pallas_call(
        functools.partial(_mixer_kernel, n_side=len(sides), n_tiles=n_tiles, tiles_per_seq=seq // tm,
                          time_ordered_input=time_ordered_input, time_ordered_output=time_ordered_output),
        grid=(n_tiles // 2 + 1,),
        in_specs=[
            pl.BlockSpec(memory_space=pl.ANY),
            _const_spec((1, d)),
            _const_spec((d, d_in)),
            _const_spec(conv_w.shape),
            _const_spec(lru_conv_w.shape),
            _const_spec((1, W_LRU)),
            _const_spec(wgate.shape),
            _const_spec((1, W_LRU)),
            _const_spec((1, W_LRU)),
            _const_spec((1, W_LRU)),
            _const_spec(w_out.shape),
            *side_in,
        ],
        out_specs=[pl.BlockSpec(memory_space=pl.ANY), *side_out],
        out_shape=[jax.ShapeDtypeStruct((t, d) if time_ordered_output else (t // SUBLANES, SUBLANES, d), F32),
                   *side_shapes],
        scratch_shapes=[
            pltpu.VMEM((6, seg, SUBLANES, d), F32),
            pltpu.VMEM((tm, d_in), F32),
            pltpu.VMEM((tm, d_in), F32),
            pltpu.VMEM((4, seg, SUBLANES, d), F32),
            pltpu.VMEM(((conv_w.shape[0] - 1) * SUBLANES, W_CONV), F32),
            pltpu.VMEM(((lru_conv_w.shape[0] - 1) * SUBLANES, W_LRU), F32),
            pltpu.VMEM((SUBLANES, W_LRU), F32),
            pltpu.SemaphoreType.DMA((6,)),
            pltpu.SemaphoreType.DMA((4,)),
        ],
        compiler_params=pltpu.CompilerParams(
            dimension_semantics=("arbitrary",), vmem_limit_bytes=VMEM_LIMIT_BYTES),
        name="mixer",
    )(x_in, row(gn), w_in, conv_w, lru_conv_w, row(lru_conv_b), wgate, row(ba), row(bx), row(lam), w_out,
      *side_arrays)
    return out.reshape(t, d), sides_bf16


def _swiglu_rows(h, wg_ref, wu_ref, wd_ref):
    d_ff = wg_ref.shape[-1]
    acc = jnp.zeros((h.shape[0], wd_ref.shape[-1]), F32)
    for c in range(d_ff // FF_CHUNK):
        cols = slice(c * FF_CHUNK, (c + 1) * FF_CHUNK)
        gate = jnp.dot(h, wg_ref[:, cols], preferred_element_type=F32)
        up = jnp.dot(h, wu_ref[:, cols], preferred_element_type=F32)
        act = (gate * _sigmoid(gate) * up).astype(BF16)
        acc = acc + jnp.dot(act, wd_ref[cols, :], preferred_element_type=F32)
    return acc


def _ffn_kernel(*refs, n_side):
    x_ref, gn_ref, wg_ref, wu_ref, wd_ref = refs[:5]
    side_refs, o_ref, side_out_refs = refs[5:5 + n_side], refs[5 + n_side], refs[6 + n_side:]
    for side_ref, side_out_ref in zip(side_refs, side_out_refs):
        side_out_ref[...] = side_ref[...].astype(BF16)
    x = x_ref[...]
    h = _rms_norm(x, gn_ref[...]).astype(BF16)
    o_ref[...] = x + _swiglu_rows(h, wg_ref, wu_ref, wd_ref)


def _ffn_dense(x2, gn, wg, wu, wd, sides):
    t, d = x2.shape
    tm = TM_FFN
    side_in, side_out, side_shapes, side_arrays = _side_cast_specs(sides, t // tm)
    out, *sides_bf16 = pl.pallas_call(
        functools.partial(_ffn_kernel, n_side=len(sides)),
        grid=(t // tm,),
        in_specs=[
            pl.BlockSpec((tm, d), lambda i: (i, 0)),
            _const_spec((1, d)),
            _const_spec(wg.shape),
            _const_spec(wu.shape),
            _const_spec(wd.shape),
            *side_in,
        ],
        out_specs=[pl.BlockSpec((tm, d), lambda i: (i, 0)), *side_out],
        out_shape=[jax.ShapeDtypeStruct(x2.shape, F32), *side_shapes],
        compiler_params=pltpu.CompilerParams(
            dimension_semantics=("arbitrary",), vmem_limit_bytes=VMEM_LIMIT_BYTES),
        name="ffn_dense",
    )(x2, gn.reshape(1, -1), wg, wu, wd, *side_arrays)
    return out, sides_bf16


def _split_bf16(a):
    hi = a.astype(BF16)
    return hi, (a - hi.astype(F32)).astype(BF16)


def _router_kernel(*refs, n_side):
    x_ref, gn_ref, wr_hi_ref, wr_lo_ref = refs[:4]
    side_refs, (meta_ref, gates_ref, counts_ref) = refs[4:4 + n_side], refs[4 + n_side:7 + n_side]
    side_out_refs, run = refs[7 + n_side:7 + 2 * n_side], refs[7 + 2 * n_side]
    i = pl.program_id(0)
    tm = x_ref.shape[0]
    for side_ref, side_out_ref in zip(side_refs, side_out_refs):
        side_out_ref[...] = side_ref[...].astype(BF16)

    @pl.when(i == 0)
    def _():
        run[...] = jnp.zeros((1, LANES), F32)

    h_hi, h_lo = _split_bf16(_rms_norm(x_ref[...], gn_ref[...]))
    logits = (jnp.dot(h_hi, wr_hi_ref[...], preferred_element_type=F32)
              + jnp.dot(h_lo, wr_hi_ref[...], preferred_element_type=F32)
              + jnp.dot(h_hi, wr_lo_ref[...], preferred_element_type=F32))
    lane = lax.broadcasted_iota(I32, (tm, LANES), 1)
    neg_inf = jnp.float32(-jnp.inf)
    logits = jnp.where(lane < N_EXPERTS, logits, neg_inf)
    m1 = jnp.max(logits, axis=-1, keepdims=True)
    i1 = jnp.min(jnp.where(logits == m1, lane, LANES), axis=-1, keepdims=True)
    rest = jnp.where(lane == i1, neg_inf, logits)
    m2 = jnp.max(rest, axis=-1, keepdims=True)
    i2 = jnp.min(jnp.where(rest == m2, lane, LANES), axis=-1, keepdims=True)
    e = jnp.exp(m2 - m1)
    g1 = 1.0 / (1.0 + e)
    g2 = e * g1

    sel1 = lane == i1
    sel2 = lane == i2
    hot = jnp.where(sel1 | sel2, 1.0, 0.0)
    rows = lax.broadcasted_iota(I32, (tm, tm), 0)
    cols = lax.broadcasted_iota(I32, (tm, tm), 1)
    strict_lower = jnp.where(cols < rows, 1.0, 0.0).astype(BF16)
    before = jnp.dot(strict_lower, hot.astype(BF16), preferred_element_type=F32) + run[...]
    rank1 = jnp.sum(jnp.where(sel1, before, 0.0), axis=-1, keepdims=True)
    rank2 = jnp.sum(jnp.where(sel2, before, 0.0), axis=-1, keepdims=True)
    run[...] = run[...] + jnp.sum(hot, axis=0, keepdims=True)

    meta = jnp.where(lane == 0, i1, jnp.where(lane == 1, i2, jnp.where(
        lane == 2, rank1.astype(I32), jnp.where(lane == 3, rank2.astype(I32), 0))))
    meta_ref[...] = jnp.transpose(meta)[0:SUBLANES, :]
    gates_ref[...] = jnp.where(lane == 0, g1, jnp.where(lane == 1, g2, 0.0))
    counts_ref[...] = jnp.broadcast_to(run[...], (SUBLANES, LANES))


def _router(x2, gn, w_router, sides):
    t, d = x2.shape
    tm = TM_ROUTE
    wr_hi, wr_lo = _split_bf16(jnp.zeros((d, LANES), F32).at[:, :N_EXPERTS].set(w_router))
    side_in, side_out, side_shapes, side_arrays = _side_cast_specs(sides, t // tm)
    meta, gates, counts, *sides_bf16 = pl.pallas_call(
        functools.partial(_router_kernel, n_side=len(sides)),
        grid=(t // tm,),
        in_specs=[
            pl.BlockSpec((tm, d), lambda i: (i, 0)),
            _const_spec((1, d)),
            _const_spec((d, LANES)),
            _const_spec((d, LANES)),
            *side_in,
        ],
        out_specs=[
            pl.BlockSpec((SUBLANES, tm), lambda i: (0, i)),
            pl.BlockSpec((tm, LANES), lambda i: (i, 0)),
            pl.BlockSpec((SUBLANES, LANES), lambda i: (0, 0)),
            *side_out,
        ],
        out_shape=[
            jax.ShapeDtypeStruct((SUBLANES, t), I32),
            jax.ShapeDtypeStruct((t, LANES), F32),
            jax.ShapeDtypeStruct((SUBLANES, LANES), F32),
            *side_shapes,
        ],
        scratch_shapes=[pltpu.VMEM((1, LANES), F32)],
        compiler_params=pltpu.CompilerParams(
            dimension_semantics=("arbitrary",), vmem_limit_bytes=VMEM_LIMIT_BYTES),
        name="router",
    )(x2, gn.reshape(1, -1), wr_hi, wr_lo, *side_arrays)
    return meta, gates, counts, sides_bf16


def _row_copy(src_ref, src_row, dst_ref, dst_row, sem):
    return pltpu.make_async_copy(src_ref.at[pl.ds(src_row, 1), :], dst_ref.at[pl.ds(dst_row, 1), :], sem)


def _dispatch_kernel(pad_ref, *refs):
    pos_refs, (x_ref, xs_hbm, ztile, sem, zsem) = refs[:TOP_K], refs[TOP_K:]
    i = pl.program_id(0)
    tile_rows = ztile.shape[0]

    @pl.when(i == 0)
    def _():
        ztile[...] = jnp.zeros(ztile.shape, F32)
        for e in range(N_EXPERTS):
            start = pad_ref[0, e]
            n = pad_ref[1, e]

            def fill(q, c):
                _row_copy(ztile, 0, xs_hbm, start + q, zsem).start()
                return c

            def drain(q, c):
                _row_copy(ztile, 0, xs_hbm, start, zsem).wait()
                return c

            lax.fori_loop(0, n, fill, 0)
            lax.fori_loop(0, n, drain, 0)

        def tail_copy(q):
            r0 = pl.multiple_of(q * tile_rows, tile_rows)
            return pltpu.make_async_copy(ztile, xs_hbm.at[pl.ds(r0, tile_rows), :], zsem)

        def fill_tail(q, c):
            tail_copy(q).start()
            return c

        def drain_tail(q, c):
            tail_copy(q).wait()
            return c

        n_valid = pad_ref[2, 0]
        n_tiles = xs_hbm.shape[0] // tile_rows
        lax.fori_loop(n_valid, n_tiles, fill_tail, 0)
        lax.fori_loop(n_valid, n_tiles, drain_tail, 0)

    def issue(q, c):
        for s in range(SUBLANES):
            for k in range(TOP_K):
                dst = pos_refs[k][0, 0, q * SUBLANES + s]
                pltpu.make_async_copy(x_ref.at[q, pl.ds(s, 1), :], xs_hbm.at[pl.ds(dst, 1), :],
                                      sem.at[k]).start(priority=k)
        return c

    n_groups = x_ref.shape[0]
    lax.fori_loop(0, n_groups, issue, 0)
    for k in range(TOP_K):
        for _ in range(n_groups * SUBLANES // tile_rows):
            pltpu.make_async_copy(ztile, xs_hbm.at[pl.ds(0, tile_rows), :], sem.at[k]).wait()


def _dispatch(x2, pos, pad_info, n_rows):
    t, d = x2.shape
    tm = TM_DISP
    n_steps = t // tm
    pos3 = pos.reshape(TOP_K * n_steps, 1, tm)
    grid_spec = pltpu.PrefetchScalarGridSpec(
        num_scalar_prefetch=1,
        grid=(n_steps,),
        in_specs=[
            *[pl.BlockSpec((1, 1, tm), lambda i, pad, k=k: (k * n_steps + i, 0, 0), memory_space=pltpu.SMEM)
              for k in range(TOP_K)],
            pl.BlockSpec((tm // SUBLANES, SUBLANES, d), lambda i, pad: (i, 0, 0)),
        ],
        out_specs=pl.BlockSpec(memory_space=pl.ANY),
        scratch_shapes=[
            pltpu.VMEM((TM_EXP, d), F32),
            pltpu.SemaphoreType.DMA((TOP_K,)),
            pltpu.SemaphoreType.DMA(()),
        ],
    )
    return pl.pallas_call(
        _dispatch_kernel,
        grid_spec=grid_spec,
        out_shape=jax.ShapeDtypeStruct((n_rows, d), F32),
        compiler_params=pltpu.CompilerParams(
            dimension_semantics=("arbitrary",), vmem_limit_bytes=VMEM_LIMIT_BYTES),
        name="dispatch",
    )(pad_info, *[pos3] * TOP_K, x2.reshape(t // SUBLANES, SUBLANES, d))


def _expert_kernel(texp_ref, nvalid_ref, xs_ref, gn_ref, wg_ref, wu_ref, wd_ref, ys_ref):
    i = pl.program_id(0)

    @pl.when(i < nvalid_ref[0])
    def _():
        h = _rms_norm(xs_ref[...], gn_ref[...]).astype(BF16)
        ys_ref[...] = _swiglu_rows(h, wg_ref, wu_ref, wd_ref)

    @pl.when(i >= nvalid_ref[0])
    def _():
        ys_ref[...] = jnp.zeros(ys_ref.shape, F32)


def _experts(xs, gn, wg, wu, wd, tile_expert, n_valid):
    n_rows, d = xs.shape
    tm = TM_EXP
    d_ff = wg.shape[-1]
    last = lambda i, nv: jnp.maximum(jnp.minimum(i, nv[0] - 1), 0)
    tile = lambda i, te, nv: (last(i, nv), 0)
    expert = lambda i, te, nv: (te[last(i, nv)], 0, 0)
    grid_spec = pltpu.PrefetchScalarGridSpec(
        num_scalar_prefetch=2,
        grid=(n_rows // tm,),
        in_specs=[
            pl.BlockSpec((tm, d), tile),
            pl.BlockSpec((1, d), lambda i, te, nv: (0, 0)),
            pl.BlockSpec((None, d, d_ff), expert),
            pl.BlockSpec((None, d, d_ff), expert),
            pl.BlockSpec((None, d_ff, d), expert),
        ],
        out_specs=pl.BlockSpec((tm, d), lambda i, te, nv: (i, 0)),
    )
    return pl.pallas_call(
        _expert_kernel,
        grid_spec=grid_spec,
        out_shape=jax.ShapeDtypeStruct((n_rows, d), F32),
        compiler_params=pltpu.CompilerParams(
            dimension_semantics=("arbitrary",), vmem_limit_bytes=VMEM_LIMIT_BYTES),
        name="experts",
    )(tile_expert, n_valid, xs, gn.reshape(1, -1), wg.astype(BF16), wu.astype(BF16), wd.astype(BF16))


def _combine_kernel(*refs):
    pos0_refs, pos_next_refs = refs[:TOP_K], refs[TOP_K:2 * TOP_K]
    x_ref, gates_ref, gf_ref, ys_rows, ys_hbm, o_ref, buf, sem = refs[2 * TOP_K:]
    i = pl.program_id(0)
    n_groups = buf.shape[2]
    tm = n_groups * SUBLANES
    d = buf.shape[4]

    def gather(pos_refs, slot):
        def issue(q, c):
            for s in range(SUBLANES):
                for k in range(TOP_K):
                    src = pos_refs[k][0, 0, q * SUBLANES + s]
                    pltpu.make_async_copy(ys_rows.at[pl.ds(src, 1), :],
                                          buf.at[slot, k, q, pl.ds(s, 1), :], sem.at[slot, k]).start(priority=k)
            return c

        lax.fori_loop(0, n_groups, issue, 0)

    @pl.when(i == 0)
    def _():
        gather(pos0_refs, 0)

    @pl.when(i + 1 < pl.num_programs(0))
    def _():
        gather(pos_next_refs, (i + 1) % 2)

    slot = i % 2
    for k in range(TOP_K):
        pltpu.make_async_copy(ys_hbm.at[pl.ds(0, n_groups)], buf.at[slot, k], sem.at[slot, k]).wait()
    gates = gates_ref[...]
    y = (x_ref[...] + gates[:, 0:1] * buf[slot, 0].reshape(tm, d) + gates[:, 1:2] * buf[slot, 1].reshape(tm, d))
    o_ref[...] = _rms_norm(y, gf_ref[...])


def _combine(x2, gates, pos, ys, g_final):
    t, d = x2.shape
    tm = TM_DISP
    n_steps = t // tm
    pos3 = pos.reshape(TOP_K * n_steps, 1, tm)
    pos_spec = lambda tile_of_step: [
        pl.BlockSpec((1, 1, tm), lambda i, k=k: (k * n_steps + tile_of_step(i), 0, 0), memory_space=pltpu.SMEM)
        for k in range(TOP_K)]
    return pl.pallas_call(
        _combine_kernel,
        grid=(n_steps,),
        in_specs=[
            *pos_spec(lambda i: 0),
            *pos_spec(lambda i: jnp.minimum(i + 1, n_steps - 1)),
            pl.BlockSpec((tm, d), lambda i: (i, 0)),
            pl.BlockSpec((tm, LANES), lambda i: (i, 0)),
            _const_spec((1, d)),
            pl.BlockSpec(memory_space=pl.ANY),
            pl.BlockSpec(memory_space=pl.ANY),
        ],
        out_specs=pl.BlockSpec((tm, d), lambda i: (i, 0)),
        out_shape=jax.ShapeDtypeStruct(x2.shape, F32),
        scratch_shapes=[pltpu.VMEM((2, TOP_K, tm // SUBLANES, SUBLANES, d), F32),
                        pltpu.SemaphoreType.DMA((2, TOP_K))],
        compiler_params=pltpu.CompilerParams(
            dimension_semantics=("arbitrary",), vmem_limit_bytes=VMEM_LIMIT_BYTES),
        name="combine",
    )(*[pos3] * (2 * TOP_K), x2, gates, g_final.reshape(1, -1), ys,
      ys.reshape(ys.shape[0] // SUBLANES, SUBLANES, d))


def _moe_layer(x2, gn, w_router, wg, wu_f32, wd, g_final):
    t, _ = x2.shape
    tm = TM_EXP
    n_tiles = (TOP_K * t) // tm + N_EXPERTS
    meta, gates, counts, (wu,) = _router(x2, gn, w_router, [wu_f32.reshape(-1, wu_f32.shape[-1])])
    wu = wu.reshape(wu_f32.shape)

    counts = counts[0, :N_EXPERTS].astype(I32)
    padded = ((counts + tm - 1) // tm) * tm
    ends = jnp.cumsum(padded)
    starts = ends - padded
    idx = meta[0:TOP_K, :]
    rank = meta[TOP_K:2 * TOP_K, :]
    group_start = sum(jnp.where(idx == e, starts[e], 0) for e in range(N_EXPERTS))
    pos = group_start + rank
    tile_start = jnp.arange(n_tiles, dtype=I32) * tm
    tile_expert = jnp.minimum(
        jnp.sum((tile_start[:, None] >= ends[None, :]).astype(I32), axis=1), N_EXPERTS - 1).astype(I32)
    n_valid = (ends[-1:] // tm).astype(I32)
    pad_info = jnp.stack([starts + counts, padded - counts, jnp.broadcast_to(n_valid, (N_EXPERTS,))]).astype(I32)

    xs = _dispatch(x2, pos, pad_info, n_tiles * tm)
    ys = _experts(xs, gn, wg, wu, wd, tile_expert, n_valid)
    return _combine(x2, gates, pos, ys, g_final)


def kernel(x, norm_mix, norm_ffn, norm_final, w_in, conv_w, lru_conv_w, lru_conv_b, lru_wa, lru_ba, lru_wx,
           lru_bx, lru_lambda, w_out, ffn_w_gate, ffn_w_up, ffn_w_down, w_router, moe_w_gate, moe_w_up,
           moe_w_down):
    bsz, seq, d = x.shape
    depth = w_in.shape[0]
    assert depth == 2, "layer 0 uses the dense FFN, layer 1 the routed FFN followed by the final norm"
    x2 = x.reshape(bsz * seq, d)

    def mixer(x2, l, w_in_l, w_out_l, sides):
        return _mixer(x2, seq, l == 0, l == depth - 1, norm_mix[l], w_in_l, conv_w[l], lru_conv_w[l],
                      lru_conv_b[l], lru_wa[l], lru_ba[l], lru_wx[l], lru_bx[l], lru_lambda[l], w_out_l, sides)

    flat = lambda w: w.reshape(-1, w.shape[-1])
    x2, (ffn_gate, ffn_up, ffn_down) = mixer(
        x2, 0, w_in[0].astype(BF16), w_out[0].astype(BF16), [ffn_w_gate[0], ffn_w_up[0], ffn_w_down[0]])
    x2, (moe_gate, moe_down, w_in_1, w_out_1) = _ffn_dense(
        x2, norm_ffn[0], ffn_gate, ffn_up, ffn_down,
        [flat(moe_w_gate[0]), flat(moe_w_down[0]), (w_in, 1), (w_out, 1)])
    x2, _ = mixer(x2, 1, w_in_1, w_out_1, [])
    x2 = _moe_layer(x2, norm_ffn[1], w_router[0], moe_gate.reshape(moe_w_gate[0].shape), moe_w_up[0],
                    moe_down.reshape(moe_w_down[0].shape), norm_final)
    return x2.reshape(bsz, seq, d)
```

```python
import functools

import jax
import jax.numpy as jnp
from jax import lax
from jax.experimental import pallas as pl
from jax.experimental.pallas import tpu as pltpu

F32 = jnp.float32
BF16 = jnp.bfloat16
I32 = jnp.int32

EPS = 1e-6
RG_C = 8.0
W_CONV = 512
W_LRU = 512
N_EXPERTS = 8
TOP_K = 2

SUBLANES = 8
LANES = 128
MXU_DIM = 256
VMEM_LIMIT_BYTES = 56 * 1024 * 1024

TM_MIX = 512
TM_FFN = 512
FF_CHUNK = 512
TM_ROUTE = 512
TM_DISP = 512
TM_EXP = 512


def _const_spec(shape):
    nd = len(shape)
    return pl.BlockSpec(shape, lambda *_: (0,) * nd, pipeline_mode=pl.Buffered(1))


def _rms_norm(x, g):
    ms = jnp.mean(x * x, axis=-1, keepdims=True)
    return x * lax.rsqrt(ms + EPS) * g


def _sigmoid(x):
    return 1.0 / (1.0 + jnp.exp(-x))


def _sigmoid_via_tanh(x):
    return 0.5 * jnp.tanh(0.5 * x) + 0.5


def _gelu_tanh(x):
    c = 0.7978845608028654
    return 0.5 * x * (1.0 + jnp.tanh(c * (x + 0.044715 * (x * x * x))))


def _segment_copies(hbm, tile, buf, slot, sem, to_hbm):
    seg = buf.shape[1]
    copies = []
    for i in range(SUBLANES):
        rows = hbm.at[pl.ds(tile * (seg * SUBLANES) + i * seg, seg), :]
        vm = buf.at[slot, :, i, :]
        copies.append(pltpu.make_async_copy(vm, rows, sem.at[slot]) if to_hbm
                      else pltpu.make_async_copy(rows, vm, sem.at[slot]))
    return copies


def _segment_halo(cur, prev, n):
    first = lax.broadcasted_iota(I32, (SUBLANES, cur.shape[1]), 0) == 0
    out = []
    for k in range(n):
        rows = slice(k * SUBLANES, (k + 1) * SUBLANES)
        out.append(jnp.where(first, pltpu.roll(prev[rows, :], 1, axis=0), pltpu.roll(cur[rows, :], 1, axis=0)))
    return jnp.concatenate(out, axis=0)


def _mixer_kernel(*refs, n_side, n_tiles, tiles_per_seq, time_ordered_input, time_ordered_output):
    (x_hbm, gn_ref, win_ref, cw_ref, lcw_ref, lcb_ref, wgate_ref, ba_ref, bx_ref, lam_ref, wout_ref), refs = (
        refs[:11], refs[11:])
    side_refs, o_hbm, side_out_refs = refs[:n_side], refs[n_side], refs[n_side + 1:2 * n_side + 1]
    xbuf, p_even, p_odd, obuf, ztail, utail, hcarry, xsem, osem = refs[2 * n_side + 1:]
    k = pl.program_id(0)
    last_step = pl.num_programs(0) - 1
    seg = xbuf.shape[1]
    tm = seg * SUBLANES
    d = xbuf.shape[3]
    n_x = xbuf.shape[0]
    t_even = 2 * k
    t_odd = 2 * k + 1

    def fetch(tile):
        slot = tile % n_x
        if time_ordered_input:
            return _segment_copies(x_hbm, tile, xbuf, slot, xsem, False)
        return [pltpu.make_async_copy(x_hbm.at[pl.ds(tile * seg, seg)], xbuf.at[slot], xsem.at[slot])]

    n_o = obuf.shape[0]

    def out_slot(tile):
        return (tile + n_o) % n_o

    def write_back(tile):
        slot = out_slot(tile)
        if time_ordered_output:
            return _segment_copies(o_hbm, tile, obuf, slot, osem, True)
        return [pltpu.make_async_copy(obuf.at[slot], o_hbm.at[pl.ds(tile * seg, seg)], osem.at[slot])]

    @pl.when(k == 0)
    def _():
        for c in fetch(0) + fetch(1):
            c.start()
        xbuf[n_x - 1] = jnp.zeros(xbuf.shape[1:], F32)
        p_odd[...] = jnp.zeros(p_odd.shape, F32)
        ztail[...] = jnp.zeros(ztail.shape, F32)
        utail[...] = jnp.zeros(utail.shape, F32)
        hcarry[...] = jnp.zeros(hcarry.shape, F32)

    @pl.when(k < last_step)
    def _():
        for c in fetch(t_even) + fetch(t_odd):
            c.wait()

    @pl.when(k + 1 < last_step)
    def _():
        for c in fetch(t_even + 2) + fetch(t_odd + 2):
            c.start()

    @pl.when(k >= 3)
    def _():
        for c in write_back(t_even - 1 - n_o):
            c.wait()

    @pl.when(k >= 2)
    def _():
        for c in write_back(t_even - n_o):
            c.wait()

    def project(tile, p_ref):
        xa = xbuf[jnp.minimum(tile, n_tiles - 1) % n_x].reshape(tm, d)
        ha = _rms_norm(xa, gn_ref[...]).astype(BF16)
        p_ref[...] = jnp.dot(ha, win_ref[...], preferred_element_type=F32)

    def finish(tile, p_ref, state):
        z_tail, u_tail, h_last = state
        fresh = (tile + tiles_per_seq) % tiles_per_seq == 0
        z_tail = jnp.where(fresh, 0.0, z_tail)
        u_tail = jnp.where(fresh, 0.0, u_tail)
        h_last = jnp.where(fresh, 0.0, h_last)
        c_gate = p_ref[:, 0:W_CONV]
        b_gate = p_ref[:, W_CONV:2 * W_CONV]
        v = p_ref[:, 2 * W_CONV:3 * W_CONV]
        u = p_ref[:, 3 * W_CONV:3 * W_CONV + W_LRU]
        g = p_ref[:, 3 * W_CONV + W_LRU:]

        z = c_gate * v
        nz = z_tail.shape[0]
        z_new_tail = z[tm - nz:, :]
        zx = jnp.concatenate([_segment_halo(z_new_tail, z_tail, nz // SUBLANES), z], axis=0)
        cw = cw_ref[...]
        conv = cw[2:3, :] * z + cw[1:2, :] * zx[nz - SUBLANES:nz - SUBLANES + tm, :] + cw[0:1, :] * zx[0:tm, :]
        y_a = b_gate * conv

        nu = u_tail.shape[0]
        u_new_tail = u[tm - nu:, :]
        ux = jnp.concatenate([_segment_halo(u_new_tail, u_tail, nu // SUBLANES), u], axis=0)
        lw = lcw_ref[...]
        uc = (lw[3:4, :] * u + lw[2:3, :] * ux[2 * SUBLANES:2 * SUBLANES + tm, :]
              + lw[1:2, :] * ux[SUBLANES:SUBLANES + tm, :] + lw[0:1, :] * ux[0:tm, :] + lcb_ref[...])

        ucb = uc.astype(BF16)
        pre0 = jnp.dot(ucb[:, 0:MXU_DIM], wgate_ref[0], preferred_element_type=F32)
        pre1 = jnp.dot(ucb[:, MXU_DIM:], wgate_ref[1], preferred_element_type=F32)
        pre_a = jnp.concatenate([pre0[:, :MXU_DIM], pre1[:, :MXU_DIM]], axis=1) + ba_ref[...]
        pre_x = jnp.concatenate([pre0[:, MXU_DIM:], pre1[:, MXU_DIM:]], axis=1) + bx_ref[...]
        r = _sigmoid_via_tanh(pre_a)
        i_gate = _sigmoid_via_tanh(pre_x)
        neg_lam = -lam_ref[...]
        softplus = jnp.maximum(neg_lam, 0.0) + jnp.log1p(jnp.exp(-jnp.abs(neg_lam)))
        log_a = (-RG_C) * r * softplus
        a = jnp.exp(log_a)
        th = jnp.tanh(log_a)
        b = jnp.sqrt((-2.0 * th) / (1.0 - th)) * (i_gate * uc)

        a3 = a.reshape(seg, SUBLANES, W_LRU)
        b3 = b.reshape(seg, SUBLANES, W_LRU)
        prod = [a3[0]]
        part = [b3[0]]
        for j in range(1, seg):
            prod.append(a3[j] * prod[j - 1])
            part.append(a3[j] * part[j - 1] + b3[j])
        init_rows = []
        for i in range(SUBLANES):
            init_rows.append(h_last)
            h_last = prod[seg - 1][i:i + 1, :] * h_last + part[seg - 1][i:i + 1, :]
        init = jnp.concatenate(init_rows, axis=0)
        hs = jnp.stack([part[j] + prod[j] * init for j in range(seg)], axis=0).reshape(tm, W_LRU)

        y_b = hs * _gelu_tanh(g)
        y = jnp.concatenate([y_a, y_b], axis=1).astype(BF16)
        xc = xbuf[(tile + n_x) % n_x].reshape(tm, d)
        out = xc + jnp.dot(y, wout_ref[...], preferred_element_type=F32)
        return out.reshape(seg, SUBLANES, d), (z_new_tail, u_new_tail, h_last)

    for side_ref, side_out_ref in zip(side_refs, side_out_refs):
        side_out_ref[...] = side_ref[...].astype(BF16)
    state = (ztail[...], utail[...], hcarry[0:1, :])
    project(t_even, p_even)
    obuf[out_slot(t_even - 1)], state = finish(t_even - 1, p_odd, state)
    project(t_odd, p_odd)
    obuf[out_slot(t_even)], state = finish(t_even, p_even, state)
    ztail[...], utail[...] = state[0], state[1]
    hcarry[...] = jnp.broadcast_to(state[2], hcarry.shape)

    @pl.when(k >= 1)
    def _():
        for c in write_back(t_even - 1):
            c.start()

    @pl.when(k < last_step)
    def _():
        for c in write_back(t_even):
            c.start()

    @pl.when(k == last_step)
    def _():
        for c in write_back(t_even - 3) + write_back(t_even - 2) + write_back(t_even - 1):
            c.wait()


def _block_diag_halves(w):
    nh, dh, _ = w.shape
    per = MXU_DIM // dh
    w4 = w.reshape(nh // per, per, dh, dh)
    eye = jnp.eye(per, dtype=w.dtype)
    return jnp.einsum("kaij,ab->kaibj", w4, eye).reshape(nh // per, MXU_DIM, MXU_DIM)


def _side_cast_specs(sides, n_blocks):
    row = lambda i: jnp.minimum(i, n_blocks - 1)
    in_specs, out_specs, shapes, arrays = [], [], [], []
    for side in sides:
        arr, layer = side if isinstance(side, tuple) else (side, None)
        rows, cols = arr.shape[-2:]
        block = (rows // n_blocks, cols)
        if layer is None:
            in_specs.append(pl.BlockSpec(block, lambda i, *_: (row(i), 0)))
        else:
            in_specs.append(pl.BlockSpec((None,) + block, lambda i, *_, layer=layer: (layer, row(i), 0)))
        out_specs.append(pl.BlockSpec(block, lambda i, *_: (row(i), 0)))
        shapes.append(jax.ShapeDtypeStruct((rows, cols), BF16))
        arrays.append(arr)
    return in_specs, out_specs, shapes, arrays


def _mixer(x2, seq, time_ordered_input, time_ordered_output, gn, w_in, conv_w, lru_conv_w, lru_conv_b, wa, ba,
           wx, bx, lam, w_out, sides):
    t, d = x2.shape
    d_in = w_in.shape[1]
    tm = TM_MIX
    seg = tm // SUBLANES
    n_tiles = t // tm
    assert n_tiles % 2 == 0 and (seq // tm) % 2 == 0
    wgate = jnp.concatenate([_block_diag_halves(wa), _block_diag_halves(wx)], axis=-1).astype(BF16)
    row = lambda a: a.reshape(1, -1)
    x_in = x2 if time_ordered_input else x2.reshape(t // SUBLANES, SUBLANES, d)
    side_in, side_out, side_shapes, side_arrays = _side_cast_specs(sides, n_tiles // 2)
    out, *sides_bf16 = pl.pallas_call(
        functools.partial(_mixer_kernel, n_side=len(sides), n_tiles=n_tiles, tiles_per_seq=seq // tm,
                          time_ordered_input=time_ordered_input, time_ordered_output=time_ordered_output),
        grid=(n_tiles // 2 + 1,),
        in_specs=[
            pl.BlockSpec(memory_space=pl.ANY),
            _const_spec((1, d)),
            _const_spec((d, d_in)),
            _const_spec(conv_w.shape),
            _const_spec(lru_conv_w.shape),
            _const_spec((1, W_LRU)),
            _const_spec(wgate.shape),
            _const_spec((1, W_LRU)),
            _const_spec((1, W_LRU)),
            _const_spec((1, W_LRU)),
            _const_spec(w_out.shape),
            *side_in,
        ],
        out_specs=[pl.BlockSpec(memory_space=pl.ANY), *side_out],
        out_shape=[jax.ShapeDtypeStruct((t, d) if time_ordered_output else (t // SUBLANES, SUBLANES, d), F32),
                   *side_shapes],
        scratch_shapes=[
            pltpu.VMEM((6, seg, SUBLANES, d), F32),
            pltpu.VMEM((tm, d_in), F32),
            pltpu.VMEM((tm, d_in), F32),
            pltpu.VMEM((4, seg, SUBLANES, d), F32),
            pltpu.VMEM(((conv_w.shape[0] - 1) * SUBLANES, W_CONV), F32),
            pltpu.VMEM(((lru_conv_w.shape[0] - 1) * SUBLANES, W_LRU), F32),
            pltpu.VMEM((SUBLANES, W_LRU), F32),
            pltpu.SemaphoreType.DMA((6,)),
            pltpu.SemaphoreType.DMA((4,)),
        ],
        compiler_params=pltpu.CompilerParams(
            dimension_semantics=("arbitrary",), vmem_limit_bytes=VMEM_LIMIT_BYTES),
        name="mixer",
    )(x_in, row(gn), w_in, conv_w, lru_conv_w, row(lru_conv_b), wgate, row(ba), row(bx), row(lam), w_out,
      *side_arrays)
    return out.reshape(t, d), sides_bf16


def _swiglu_rows(h, wg_ref, wu_ref, wd_ref):
    d_ff = wg_ref.shape[-1]
    acc = jnp.zeros((h.shape[0], wd_ref.shape[-1]), F32)
    for c in range(d_ff // FF_CHUNK):
        cols = slice(c * FF_CHUNK, (c + 1) * FF_CHUNK)
        gate = jnp.dot(h, wg_ref[:, cols], preferred_element_type=F32)
        up = jnp.dot(h, wu_ref[:, cols], preferred_element_type=F32)
        act = (gate * _sigmoid(gate) * up).astype(BF16)
        acc = acc + jnp.dot(act, wd_ref[cols, :], preferred_element_type=F32)
    return acc


def _ffn_kernel(*refs, n_side):
    x_ref, gn_ref, wg_ref, wu_ref, wd_ref = refs[:5]
    side_refs, o_ref, side_out_refs = refs[5:5 + n_side], refs[5 + n_side], refs[6 + n_side:]
    for side_ref, side_out_ref in zip(side_refs, side_out_refs):
        side_out_ref[...] = side_ref[...].astype(BF16)
    x = x_ref[...]
    h = _rms_norm(x, gn_ref[...]).astype(BF16)
    o_ref[...] = x + _swiglu_rows(h, wg_ref, wu_ref, wd_ref)


def _ffn_dense(x2, gn, wg, wu, wd, sides):
    t, d = x2.shape
    tm = TM_FFN
    side_in, side_out, side_shapes, side_arrays = _side_cast_specs(sides, t // tm)
    out, *sides_bf16 = pl.pallas_call(
        functools.partial(_ffn_kernel, n_side=len(sides)),
        grid=(t // tm,),
        in_specs=[
            pl.BlockSpec((tm, d), lambda i: (i, 0)),
            _const_spec((1, d)),
            _const_spec(wg.shape),
            _const_spec(wu.shape),
            _const_spec(wd.shape),
            *side_in,
        ],
        out_specs=[pl.BlockSpec((tm, d), lambda i: (i, 0)), *side_out],
        out_shape=[jax.ShapeDtypeStruct(x2.shape, F32), *side_shapes],
        compiler_params=pltpu.CompilerParams(
            dimension_semantics=("arbitrary",), vmem_limit_bytes=VMEM_LIMIT_BYTES),
        name="ffn_dense",
    )(x2, gn.reshape(1, -1), wg, wu, wd, *side_arrays)
    return out, sides_bf16


def _split_bf16(a):
    hi = a.astype(BF16)
    return hi, (a - hi.astype(F32)).astype(BF16)


def _router_kernel(*refs, n_side):
    x_ref, gn_ref, wr_hi_ref, wr_lo_ref = refs[:4]
    side_refs, (meta_ref, gates_ref, counts_ref) = refs[4:4 + n_side], refs[4 + n_side:7 + n_side]
    side_out_refs, run = refs[7 + n_side:7 + 2 * n_side], refs[7 + 2 * n_side]
    i = pl.program_id(0)
    tm = x_ref.shape[0]
    for side_ref, side_out_ref in zip(side_refs, side_out_refs):
        side_out_ref[...] = side_ref[...].astype(BF16)

    @pl.when(i == 0)
    def _():
        run[...] = jnp.zeros((1, LANES), F32)

    h_hi, h_lo = _split_bf16(_rms_norm(x_ref[...], gn_ref[...]))
    logits = (jnp.dot(h_hi, wr_hi_ref[...], preferred_element_type=F32)
              + jnp.dot(h_lo, wr_hi_ref[...], preferred_element_type=F32)
              + jnp.dot(h_hi, wr_lo_ref[...], preferred_element_type=F32))
    lane = lax.broadcasted_iota(I32, (tm, LANES), 1)
    neg_inf = jnp.float32(-jnp.inf)
    logits = jnp.where(lane < N_EXPERTS, logits, neg_inf)
    m1 = jnp.max(logits, axis=-1, keepdims=True)
    i1 = jnp.min(jnp.where(logits == m1, lane, LANES), axis=-1, keepdims=True)
    rest = jnp.where(lane == i1, neg_inf, logits)
    m2 = jnp.max(rest, axis=-1, keepdims=True)
    i2 = jnp.min(jnp.where(rest == m2, lane, LANES), axis=-1, keepdims=True)
    e = jnp.exp(m2 - m1)
    g1 = 1.0 / (1.0 + e)
    g2 = e * g1

    sel1 = lane == i1
    sel2 = lane == i2
    hot = jnp.where(sel1 | sel2, 1.0, 0.0)
    rows = lax.broadcasted_iota(I32, (tm, tm), 0)
    cols = lax.broadcasted_iota(I32, (tm, tm), 1)
    strict_lower = jnp.where(cols < rows, 1.0, 0.0).astype(BF16)
    before = jnp.dot(strict_lower, hot.astype(BF16), preferred_element_type=F32) + run[...]
    rank1 = jnp.sum(jnp.where(sel1, before, 0.0), axis=-1, keepdims=True)
    rank2 = jnp.sum(jnp.where(sel2, before, 0.0), axis=-1, keepdims=True)
    run[...] = run[...] + jnp.sum(hot, axis=0, keepdims=True)

    meta = jnp.where(lane == 0, i1, jnp.where(lane == 1, i2, jnp.where(
        lane == 2, rank1.astype(I32), jnp.where(lane == 3, rank2.astype(I32), 0))))
    meta_ref[...] = jnp.transpose(meta)[0:SUBLANES, :]
    gates_ref[...] = jnp.where(lane == 0, g1, jnp.where(lane == 1, g2, 0.0))
    counts_ref[...] = jnp.broadcast_to(run[...], (SUBLANES, LANES))


def _router(x2, gn, w_router, sides):
    t, d = x2.shape
    tm = TM_ROUTE
    wr_hi, wr_lo = _split_bf16(jnp.zeros((d, LANES), F32).at[:, :N_EXPERTS].set(w_router))
    side_in, side_out, side_shapes, side_arrays = _side_cast_specs(sides, t // tm)
    meta, gates, counts, *sides_bf16 = pl.pallas_call(
        functools.partial(_router_kernel, n_side=len(sides)),
        grid=(t // tm,),
        in_specs=[
            pl.BlockSpec((tm, d), lambda i: (i, 0)),
            _const_spec((1, d)),
            _const_spec((d, LANES)),
            _const_spec((d, LANES)),
            *side_in,
        ],
        out_specs=[
            pl.BlockSpec((SUBLANES, tm), lambda i: (0, i)),
            pl.BlockSpec((tm, LANES), lambda i: (i, 0)),
            pl.BlockSpec((SUBLANES, LANES), lambda i: (0, 0)),
            *side_out,
        ],
        out_shape=[
            jax.ShapeDtypeStruct((SUBLANES, t), I32),
            jax.ShapeDtypeStruct((t, LANES), F32),
            jax.ShapeDtypeStruct((SUBLANES, LANES), F32),
            *side_shapes,
        ],
        scratch_shapes=[pltpu.VMEM((1, LANES), F32)],
        compiler_params=pltpu.CompilerParams(
            dimension_semantics=("arbitrary",), vmem_limit_bytes=VMEM_LIMIT_BYTES),
        name="router",
    )(x2, gn.reshape(1, -1), wr_hi, wr_lo, *side_arrays)
    return meta, gates, counts, sides_bf16


def _row_copy(src_ref, src_row, dst_ref, dst_row, sem):
    return pltpu.make_async_copy(src_ref.at[pl.ds(src_row, 1), :], dst_ref.at[pl.ds(dst_row, 1), :], sem)


def _dispatch_kernel(pad_ref, *refs, n_side):
    pos_refs, x_ref, refs = refs[:TOP_K], refs[TOP_K], refs[TOP_K + 1:]
    side_refs, xs_hbm, side_out_refs = refs[:n_side], refs[n_side], refs[n_side + 1:2 * n_side + 1]
    ztile, sem, zsem = refs[2 * n_side + 1:]
    i = pl.program_id(0)
    tile_rows = ztile.shape[0]

    @pl.when(i == 0)
    def _():
        ztile[...] = jnp.zeros(ztile.shape, F32)
        for e in range(N_EXPERTS):
            start = pad_ref[0, e]
            n = pad_ref[1, e]

            def fill(q, c):
                _row_copy(ztile, 0, xs_hbm, start + q, zsem).start()
                return c

            def drain(q, c):
                _row_copy(ztile, 0, xs_hbm, start, zsem).wait()
                return c

            lax.fori_loop(0, n, fill, 0)
            lax.fori_loop(0, n, drain, 0)

        def tail_copy(q):
            r0 = pl.multiple_of(q * tile_rows, tile_rows)
            return pltpu.make_async_copy(ztile, xs_hbm.at[pl.ds(r0, tile_rows), :], zsem)

        def fill_tail(q, c):
            tail_copy(q).start()
            return c

        def drain_tail(q, c):
            tail_copy(q).wait()
            return c

        n_valid = pad_ref[2, 0]
        n_tiles = xs_hbm.shape[0] // tile_rows
        lax.fori_loop(n_valid, n_tiles, fill_tail, 0)
        lax.fori_loop(n_valid, n_tiles, drain_tail, 0)

    def issue(q, c):
        for s in range(SUBLANES):
            for k in range(TOP_K):
                dst = pos_refs[k][0, 0, q * SUBLANES + s]
                pltpu.make_async_copy(x_ref.at[q, pl.ds(s, 1), :], xs_hbm.at[pl.ds(dst, 1), :],
                                      sem.at[k]).start(priority=k)
        return c

    n_groups = x_ref.shape[0]
    lax.fori_loop(0, n_groups, issue, 0)
    for side_ref, side_out_ref in zip(side_refs, side_out_refs):
        side_out_ref[...] = side_ref[...].astype(BF16)
    for k in range(TOP_K):
        for _ in range(n_groups * SUBLANES // tile_rows):
            pltpu.make_async_copy(ztile, xs_hbm.at[pl.ds(0, tile_rows), :], sem.at[k]).wait()


def _dispatch(x2, pos, pad_info, n_rows, sides):
    t, d = x2.shape
    tm = TM_DISP
    n_steps = t // tm
    pos3 = pos.reshape(TOP_K * n_steps, 1, tm)
    side_in, side_out, side_shapes, side_arrays = _side_cast_specs(sides, n_steps)
    grid_spec = pltpu.PrefetchScalarGridSpec(
        num_scalar_prefetch=1,
        grid=(n_steps,),
        in_specs=[
            *[pl.BlockSpec((1, 1, tm), lambda i, pad, k=k: (k * n_steps + i, 0, 0), memory_space=pltpu.SMEM)
              for k in range(TOP_K)],
            pl.BlockSpec((tm // SUBLANES, SUBLANES, d), lambda i, pad: (i, 0, 0)),
            *side_in,
        ],
        out_specs=[pl.BlockSpec(memory_space=pl.ANY), *side_out],
        scratch_shapes=[
            pltpu.VMEM((TM_EXP, d), F32),
            pltpu.SemaphoreType.DMA((TOP_K,)),
            pltpu.SemaphoreType.DMA(()),
        ],
    )
    xs, *sides_bf16 = pl.pallas_call(
        functools.partial(_dispatch_kernel, n_side=len(sides)),
        grid_spec=grid_spec,
        out_shape=[jax.ShapeDtypeStruct((n_rows, d), F32), *side_shapes],
        compiler_params=pltpu.CompilerParams(
            dimension_semantics=("arbitrary",), vmem_limit_bytes=VMEM_LIMIT_BYTES),
        name="dispatch",
    )(pad_info, *[pos3] * TOP_K, x2.reshape(t // SUBLANES, SUBLANES, d), *side_arrays)
    return xs, sides_bf16


def _expert_kernel(tinfo_ref, nvalid_ref, xs_ref, gn_ref, wg_ref, wu_ref, wd_ref, ys_ref):
    i = pl.program_id(0)
    tm = xs_ref.shape[0]
    half = tm // 2
    live = i < nvalid_ref[0]
    rows = tinfo_ref[1, i]

    @pl.when(live & (rows > half))
    def _():
        h = _rms_norm(xs_ref[...], gn_ref[...]).astype(BF16)
        ys_ref[...] = _swiglu_rows(h, wg_ref, wu_ref, wd_ref)

    @pl.when(live & (rows <= half))
    def _():
        h = _rms_norm(xs_ref[0:half, :], gn_ref[...]).astype(BF16)
        ys_ref[0:half, :] = _swiglu_rows(h, wg_ref, wu_ref, wd_ref)
        ys_ref[half:, :] = jnp.zeros((tm - half, ys_ref.shape[1]), F32)

    @pl.when(jnp.logical_not(live))
    def _():
        ys_ref[...] = jnp.zeros(ys_ref.shape, F32)


def _experts(xs, gn, wg, wu, wd, tile_info, n_valid):
    n_rows, d = xs.shape
    tm = TM_EXP
    d_ff = wg.shape[-1]
    last = lambda i, nv: jnp.maximum(jnp.minimum(i, nv[0] - 1), 0)
    tile = lambda i, ti, nv: (last(i, nv), 0)
    expert = lambda i, ti, nv: (ti[0, last(i, nv)], 0, 0)
    grid_spec = pltpu.PrefetchScalarGridSpec(
        num_scalar_prefetch=2,
        grid=(n_rows // tm,),
        in_specs=[
            pl.BlockSpec((tm, d), tile),
            pl.BlockSpec((1, d), lambda i, ti, nv: (0, 0)),
            pl.BlockSpec((None, d, d_ff), expert),
            pl.BlockSpec((None, d, d_ff), expert),
            pl.BlockSpec((None, d_ff, d), expert),
        ],
        out_specs=pl.BlockSpec((tm, d), lambda i, ti, nv: (i, 0)),
    )
    return pl.pallas_call(
        _expert_kernel,
        grid_spec=grid_spec,
        out_shape=jax.ShapeDtypeStruct((n_rows, d), F32),
        compiler_params=pltpu.CompilerParams(
            dimension_semantics=("arbitrary",), vmem_limit_bytes=VMEM_LIMIT_BYTES),
        name="experts",
    )(tile_info, n_valid, xs, gn.reshape(1, -1), wg, wu, wd)


def _combine_kernel(*refs):
    pos0_refs, pos_next_refs = refs[:TOP_K], refs[TOP_K:2 * TOP_K]
    x_ref, gates_ref, gf_ref, ys_rows, ys_hbm, o_ref, buf, sem = refs[2 * TOP_K:]
    i = pl.program_id(0)
    n_groups = buf.shape[2]
    tm = n_groups * SUBLANES
    d = buf.shape[4]

    def gather(pos_refs, slot):
        def issue(q, c):
            for s in range(SUBLANES):
                for k in range(TOP_K):
                    src = pos_refs[k][0, 0, q * SUBLANES + s]
                    pltpu.make_async_copy(ys_rows.at[pl.ds(src, 1), :],
                                          buf.at[slot, k, q, pl.ds(s, 1), :], sem.at[slot, k]).start(priority=k)
            return c

        lax.fori_loop(0, n_groups, issue, 0)

    @pl.when(i == 0)
    def _():
        gather(pos0_refs, 0)

    @pl.when(i + 1 < pl.num_programs(0))
    def _():
        gather(pos_next_refs, (i + 1) % 2)

    slot = i % 2
    for k in range(TOP_K):
        pltpu.make_async_copy(ys_hbm.at[pl.ds(0, n_groups)], buf.at[slot, k], sem.at[slot, k]).wait()
    gates = gates_ref[...]
    y = (x_ref[...] + gates[:, 0:1] * buf[slot, 0].reshape(tm, d) + gates[:, 1:2] * buf[slot, 1].reshape(tm, d))
    o_ref[...] = _rms_norm(y, gf_ref[...])


def _combine(x2, gates, pos, ys, g_final):
    t, d = x2.shape
    tm = TM_DISP
    n_steps = t // tm
    pos3 = pos.reshape(TOP_K * n_steps, 1, tm)
    pos_spec = lambda tile_of_step: [
        pl.BlockSpec((1, 1, tm), lambda i, k=k: (k * n_steps + tile_of_step(i), 0, 0), memory_space=pltpu.SMEM)
        for k in range(TOP_K)]
    return pl.pallas_call(
        _combine_kernel,
        grid=(n_steps,),
        in_specs=[
            *pos_spec(lambda i: 0),
            *pos_spec(lambda i: jnp.minimum(i + 1, n_steps - 1)),
            pl.BlockSpec((tm, d), lambda i: (i, 0)),
            pl.BlockSpec((tm, LANES), lambda i: (i, 0)),
            _const_spec((1, d)),
            pl.BlockSpec(memory_space=pl.ANY),
            pl.BlockSpec(memory_space=pl.ANY),
        ],
        out_specs=pl.BlockSpec((tm, d), lambda i: (i, 0)),
        out_shape=jax.ShapeDtypeStruct(x2.shape, F32),
        scratch_shapes=[pltpu.VMEM((2, TOP_K, tm // SUBLANES, SUBLANES, d), F32),
                        pltpu.SemaphoreType.DMA((2, TOP_K))],
        compiler_params=pltpu.CompilerParams(
            dimension_semantics=("arbitrary",), vmem_limit_bytes=VMEM_LIMIT_BYTES),
        name="combine",
    )(*[pos3] * (2 * TOP_K), x2, gates, g_final.reshape(1, -1), ys,
      ys.reshape(ys.shape[0] // SUBLANES, SUBLANES, d))


def _moe_layer(x2, gn, w_router, wg, wu_f32, wd, g_final):
    t, _ = x2.shape
    tm = TM_EXP
    n_tiles = (TOP_K * t) // tm + N_EXPERTS
    meta, gates, counts, _ = _router(x2, gn, w_router, [])

    counts = counts[0, :N_EXPERTS].astype(I32)
    padded = ((counts + tm - 1) // tm) * tm
    ends = jnp.cumsum(padded)
    starts = ends - padded
    idx = meta[0:TOP_K, :]
    rank = meta[TOP_K:2 * TOP_K, :]
    group_start = sum(jnp.where(idx == e, starts[e], 0) for e in range(N_EXPERTS))
    pos = group_start + rank
    tile_start = jnp.arange(n_tiles, dtype=I32) * tm
    tile_expert = jnp.minimum(
        jnp.sum((tile_start[:, None] >= ends[None, :]).astype(I32), axis=1), N_EXPERTS - 1).astype(I32)
    token_end = sum(jnp.where(tile_expert == e, starts[e] + counts[e], 0) for e in range(N_EXPERTS))
    tile_info = jnp.stack([tile_expert, jnp.clip(token_end - tile_start, 0, tm)]).astype(I32)
    n_valid = (ends[-1:] // tm).astype(I32)
    pad_info = jnp.stack([starts + counts, padded - counts, jnp.broadcast_to(n_valid, (N_EXPERTS,))]).astype(I32)

    xs, (wu,) = _dispatch(x2, pos, pad_info, n_tiles * tm, [wu_f32.reshape(-1, wu_f32.shape[-1])])
    ys = _experts(xs, gn, wg, wu.reshape(wu_f32.shape), wd, tile_info, n_valid)
    return _combine(x2, gates, pos, ys, g_final)


def kernel(x, norm_mix, norm_ffn, norm_final, w_in, conv_w, lru_conv_w, lru_conv_b, lru_wa, lru_ba, lru_wx,
           lru_bx, lru_lambda, w_out, ffn_w_gate, ffn_w_up, ffn_w_down, w_router, moe_w_gate, moe_w_up,
           moe_w_down):
    bsz, seq, d = x.shape
    depth = w_in.shape[0]
    assert depth == 2, "layer 0 uses the dense FFN, layer 1 the routed FFN followed by the final norm"
    x2 = x.reshape(bsz * seq, d)

    def mixer(x2, l, w_in_l, w_out_l, sides):
        return _mixer(x2, seq, l == 0, l == depth - 1, norm_mix[l], w_in_l, conv_w[l], lru_conv_w[l],
                      lru_conv_b[l], lru_wa[l], lru_ba[l], lru_wx[l], lru_bx[l], lru_lambda[l], w_out_l, sides)

    flat = lambda w: w.reshape(-1, w.shape[-1])
    x2, (ffn_gate, ffn_up, ffn_down) = mixer(
        x2, 0, w_in[0].astype(BF16), w_out[0].astype(BF16), [ffn_w_gate[0], ffn_w_up[0], ffn_w_down[0]])
    x2, (moe_gate, moe_down, w_in_1, w_out_1) = _ffn_dense(
        x2, norm_ffn[0], ffn_gate, ffn_up, ffn_down,
        [flat(moe_w_gate[0]), flat(moe_w_down[0]), (w_in, 1), (w_out, 1)])
    x2, _ = mixer(x2, 1, w_in_1, w_out_1, [])
    x2 = _moe_layer(x2, norm_ffn[1], w_router[0], moe_gate.reshape(moe_w_gate[0].shape), moe_w_up[0],
                    moe_down.reshape(moe_w_down[0].shape), norm_final)
    return x2.reshape(bsz, seq, d)
```

```python
import functools

import jax
import jax.numpy as jnp
from jax import lax
from jax.experimental import pallas as pl
from jax.experimental.pallas import tpu as pltpu

F32 = jnp.float32
BF16 = jnp.bfloat16
I32 = jnp.int32

EPS = 1e-6
RG_C = 8.0
W_CONV = 512
W_LRU = 512
N_EXPERTS = 8
TOP_K = 2

SUBLANES = 8
LANES = 128
MXU_DIM = 256
VMEM_LIMIT_BYTES = 56 * 1024 * 1024

TM_MIX = 512
TM_FFN = 512
FF_CHUNK = 512
TM_ROUTE = 512
TM_DISP = 1024
TM_EXP = 512


def _const_spec(shape):
    nd = len(shape)
    return pl.BlockSpec(shape, lambda *_: (0,) * nd, pipeline_mode=pl.Buffered(1))


def _rms_norm(x, g):
    ms = jnp.mean(x * x, axis=-1, keepdims=True)
    return x * lax.rsqrt(ms + EPS) * g


def _sigmoid(x):
    return 1.0 / (1.0 + jnp.exp(-x))


def _sigmoid_via_tanh(x):
    return 0.5 * jnp.tanh(0.5 * x) + 0.5


def _gelu_tanh(x):
    c = 0.7978845608028654
    return 0.5 * x * (1.0 + jnp.tanh(c * (x + 0.044715 * (x * x * x))))


def _segment_copies(hbm, tile, buf, slot, sem, to_hbm):
    seg = buf.shape[1]
    copies = []
    for i in range(SUBLANES):
        rows = hbm.at[pl.ds(tile * (seg * SUBLANES) + i * seg, seg), :]
        vm = buf.at[slot, :, i, :]
        copies.append(pltpu.make_async_copy(vm, rows, sem.at[slot]) if to_hbm
                      else pltpu.make_async_copy(rows, vm, sem.at[slot]))
    return copies


def _segment_halo(cur, prev, n):
    first = lax.broadcasted_iota(I32, (SUBLANES, cur.shape[1]), 0) == 0
    out = []
    for k in range(n):
        rows = slice(k * SUBLANES, (k + 1) * SUBLANES)
        out.append(jnp.where(first, pltpu.roll(prev[rows, :], 1, axis=0), pltpu.roll(cur[rows, :], 1, axis=0)))
    return jnp.concatenate(out, axis=0)


def _mixer_kernel(*refs, n_side, n_tiles, tiles_per_seq, time_ordered_input, time_ordered_output):
    (x_hbm, gn_ref, win_ref, cw_ref, lcw_ref, lcb_ref, wgate_ref, ba_ref, bx_ref, lam_ref, wout_ref), refs = (
        refs[:11], refs[11:])
    side_refs, o_hbm, side_out_refs = refs[:n_side], refs[n_side], refs[n_side + 1:2 * n_side + 1]
    xbuf, p_even, p_odd, obuf, ztail, utail, hcarry, xsem, osem = refs[2 * n_side + 1:]
    k = pl.program_id(0)
    last_step = pl.num_programs(0) - 1
    seg = xbuf.shape[1]
    tm = seg * SUBLANES
    d = xbuf.shape[3]
    n_x = xbuf.shape[0]
    t_even = 2 * k
    t_odd = 2 * k + 1

    def fetch(tile):
        slot = tile % n_x
        if time_ordered_input:
            return _segment_copies(x_hbm, tile, xbuf, slot, xsem, False)
        return [pltpu.make_async_copy(x_hbm.at[pl.ds(tile * seg, seg)], xbuf.at[slot], xsem.at[slot])]

    n_o = obuf.shape[0]

    def out_slot(tile):
        return (tile + n_o) % n_o

    def write_back(tile):
        slot = out_slot(tile)
        if time_ordered_output:
            return _segment_copies(o_hbm, tile, obuf, slot, osem, True)
        return [pltpu.make_async_copy(obuf.at[slot], o_hbm.at[pl.ds(tile * seg, seg)], osem.at[slot])]

    @pl.when(k == 0)
    def _():
        for c in fetch(0) + fetch(1):
            c.start()
        xbuf[n_x - 1] = jnp.zeros(xbuf.shape[1:], F32)
        p_odd[...] = jnp.zeros(p_odd.shape, F32)
        ztail[...] = jnp.zeros(ztail.shape, F32)
        utail[...] = jnp.zeros(utail.shape, F32)
        hcarry[...] = jnp.zeros(hcarry.shape, F32)

    @pl.when(k < last_step)
    def _():
        for c in fetch(t_even) + fetch(t_odd):
            c.wait()

    @pl.when(k + 1 < last_step)
    def _():
        for c in fetch(t_even + 2) + fetch(t_odd + 2):
            c.start()

    @pl.when(k >= 3)
    def _():
        for c in write_back(t_even - 1 - n_o):
            c.wait()

    @pl.when(k >= 2)
    def _():
        for c in write_back(t_even - n_o):
            c.wait()

    def project(tile, p_ref):
        xa = xbuf[jnp.minimum(tile, n_tiles - 1) % n_x].reshape(tm, d)
        ha = _rms_norm(xa, gn_ref[...]).astype(BF16)
        p_ref[...] = jnp.dot(ha, win_ref[...], preferred_element_type=F32)

    def finish(tile, p_ref, state):
        z_tail, u_tail, h_last = state
        fresh = (tile + tiles_per_seq) % tiles_per_seq == 0
        z_tail = jnp.where(fresh, 0.0, z_tail)
        u_tail = jnp.where(fresh, 0.0, u_tail)
        h_last = jnp.where(fresh, 0.0, h_last)
        c_gate = p_ref[:, 0:W_CONV]
        b_gate = p_ref[:, W_CONV:2 * W_CONV]
        v = p_ref[:, 2 * W_CONV:3 * W_CONV]
        u = p_ref[:, 3 * W_CONV:3 * W_CONV + W_LRU]
        g = p_ref[:, 3 * W_CONV + W_LRU:]

        z = c_gate * v
        nz = z_tail.shape[0]
        z_new_tail = z[tm - nz:, :]
        zx = jnp.concatenate([_segment_halo(z_new_tail, z_tail, nz // SUBLANES), z], axis=0)
        cw = cw_ref[...]
        conv = cw[2:3, :] * z + cw[1:2, :] * zx[nz - SUBLANES:nz - SUBLANES + tm, :] + cw[0:1, :] * zx[0:tm, :]
        y_a = b_gate * conv

        nu = u_tail.shape[0]
        u_new_tail = u[tm - nu:, :]
        ux = jnp.concatenate([_segment_halo(u_new_tail, u_tail, nu // SUBLANES), u], axis=0)
        lw = lcw_ref[...]
        uc = (lw[3:4, :] * u + lw[2:3, :] * ux[2 * SUBLANES:2 * SUBLANES + tm, :]
              + lw[1:2, :] * ux[SUBLANES:SUBLANES + tm, :] + lw[0:1, :] * ux[0:tm, :] + lcb_ref[...])

        ucb = uc.astype(BF16)
        pre0 = jnp.dot(ucb[:, 0:MXU_DIM], wgate_ref[0], preferred_element_type=F32)
        pre1 = jnp.dot(ucb[:, MXU_DIM:], wgate_ref[1], preferred_element_type=F32)
        pre_a = jnp.concatenate([pre0[:, :MXU_DIM], pre1[:, :MXU_DIM]], axis=1) + ba_ref[...]
        pre_x = jnp.concatenate([pre0[:, MXU_DIM:], pre1[:, MXU_DIM:]], axis=1) + bx_ref[...]
        r = _sigmoid_via_tanh(pre_a)
        i_gate = _sigmoid_via_tanh(pre_x)
        neg_lam = -lam_ref[...]
        softplus = jnp.maximum(neg_lam, 0.0) + jnp.log1p(jnp.exp(-jnp.abs(neg_lam)))
        log_a = (-RG_C) * r * softplus
        a = jnp.exp(log_a)
        th = jnp.tanh(log_a)
        b = jnp.sqrt((-2.0 * th) / (1.0 - th)) * (i_gate * uc)

        a3 = a.reshape(seg, SUBLANES, W_LRU)
        b3 = b.reshape(seg, SUBLANES, W_LRU)
        prod = [a3[0]]
        part = [b3[0]]
        for j in range(1, seg):
            prod.append(a3[j] * prod[j - 1])
            part.append(a3[j] * part[j - 1] + b3[j])
        init_rows = []
        for i in range(SUBLANES):
            init_rows.append(h_last)
            h_last = prod[seg - 1][i:i + 1, :] * h_last + part[seg - 1][i:i + 1, :]
        init = jnp.concatenate(init_rows, axis=0)
        hs = jnp.stack([part[j] + prod[j] * init for j in range(seg)], axis=0).reshape(tm, W_LRU)

        y_b = hs * _gelu_tanh(g)
        y = jnp.concatenate([y_a, y_b], axis=1).astype(BF16)
        xc = xbuf[(tile + n_x) % n_x].reshape(tm, d)
        out = xc + jnp.dot(y, wout_ref[...], preferred_element_type=F32)
        return out.reshape(seg, SUBLANES, d), (z_new_tail, u_new_tail, h_last)

    for side_ref, side_out_ref in zip(side_refs, side_out_refs):
        side_out_ref[...] = side_ref[...].astype(BF16)
    state = (ztail[...], utail[...], hcarry[0:1, :])
    project(t_even, p_even)
    obuf[out_slot(t_even - 1)], state = finish(t_even - 1, p_odd, state)
    project(t_odd, p_odd)
    obuf[out_slot(t_even)], state = finish(t_even, p_even, state)
    ztail[...], utail[...] = state[0], state[1]
    hcarry[...] = jnp.broadcast_to(state[2], hcarry.shape)

    @pl.when(k >= 1)
    def _():
        for c in write_back(t_even - 1):
            c.start()

    @pl.when(k < last_step)
    def _():
        for c in write_back(t_even):
            c.start()

    @pl.when(k == last_step)
    def _():
        for c in write_back(t_even - 3) + write_back(t_even - 2) + write_back(t_even - 1):
            c.wait()


def _block_diag_halves(w):
    nh, dh, _ = w.shape
    per = MXU_DIM // dh
    w4 = w.reshape(nh // per, per, dh, dh)
    eye = jnp.eye(per, dtype=w.dtype)
    return jnp.einsum("kaij,ab->kaibj", w4, eye).reshape(nh // per, MXU_DIM, MXU_DIM)


def _side_cast_specs(sides, n_blocks):
    row = lambda i: jnp.minimum(i, n_blocks - 1)
    in_specs, out_specs, shapes, arrays = [], [], [], []
    for side in sides:
        arr, layer = side if isinstance(side, tuple) else (side, None)
        rows, cols = arr.shape[-2:]
        block = (rows // n_blocks, cols)
        if layer is None:
            in_specs.append(pl.BlockSpec(block, lambda i, *_: (row(i), 0)))
        else:
            in_specs.append(pl.BlockSpec((None,) + block, lambda i, *_, layer=layer: (layer, row(i), 0)))
        out_specs.append(pl.BlockSpec(block, lambda i, *_: (row(i), 0)))
        shapes.append(jax.ShapeDtypeStruct((rows, cols), BF16))
        arrays.append(arr)
    return in_specs, out_specs, shapes, arrays


def _mixer(x2, seq, time_ordered_input, time_ordered_output, gn, w_in, conv_w, lru_conv_w, lru_conv_b, wa, ba,
           wx, bx, lam, w_out, sides):
    t, d = x2.shape
    d_in = w_in.shape[1]
    tm = TM_MIX
    seg = tm // SUBLANES
    n_tiles = t // tm
    assert n_tiles % 2 == 0 and (seq // tm) % 2 == 0
    wgate = jnp.concatenate([_block_diag_halves(wa), _block_diag_halves(wx)], axis=-1).astype(BF16)
    row = lambda a: a.reshape(1, -1)
    x_in = x2 if time_ordered_input else x2.reshape(t // SUBLANES, SUBLANES, d)
    side_in, side_out, side_shapes, side_arrays = _side_cast_specs(sides, n_tiles // 2)
    out, *sides_bf16 = pl.pallas_call(
        functools.partial(_mixer_kernel, n_side=len(sides), n_tiles=n_tiles, tiles_per_seq=seq // tm,
                          time_ordered_input=time_ordered_input, time_ordered_output=time_ordered_output),
        grid=(n_tiles // 2 + 1,),
        in_specs=[
            pl.BlockSpec(memory_space=pl.ANY),
            _const_spec((1, d)),
            _const_spec((d, d_in)),
            _const_spec(conv_w.shape),
            _const_spec(lru_conv_w.shape),
            _const_spec((1, W_LRU)),
            _const_spec(wgate.shape),
            _const_spec((1, W_LRU)),
            _const_spec((1, W_LRU)),
            _const_spec((1, W_LRU)),
            _const_spec(w_out.shape),
            *side_in,
        ],
        out_specs=[pl.BlockSpec(memory_space=pl.ANY), *side_out],
        out_shape=[jax.ShapeDtypeStruct((t, d) if time_ordered_output else (t // SUBLANES, SUBLANES, d), F32),
                   *side_shapes],
        scratch_shapes=[
            pltpu.VMEM((6, seg, SUBLANES, d), F32),
            pltpu.VMEM((tm, d_in), F32),
            pltpu.VMEM((tm, d_in), F32),
            pltpu.VMEM((4, seg, SUBLANES, d), F32),
            pltpu.VMEM(((conv_w.shape[0] - 1) * SUBLANES, W_CONV), F32),
            pltpu.VMEM(((lru_conv_w.shape[0] - 1) * SUBLANES, W_LRU), F32),
            pltpu.VMEM((SUBLANES, W_LRU), F32),
            pltpu.SemaphoreType.DMA((6,)),
            pltpu.SemaphoreType.DMA((4,)),
        ],
        compiler_params=pltpu.CompilerParams(
            dimension_semantics=("arbitrary",), vmem_limit_bytes=VMEM_LIMIT_BYTES),
        name="mixer",
    )(x_in, row(gn), w_in, conv_w, lru_conv_w, row(lru_conv_b), wgate, row(ba), row(bx), row(lam), w_out,
      *side_arrays)
    return out.reshape(t, d), sides_bf16


def _swiglu_rows(h, wg_ref, wu_ref, wd_ref):
    d_ff = wg_ref.shape[-1]
    acc = jnp.zeros((h.shape[0], wd_ref.shape[-1]), F32)
    for c in range(d_ff // FF_CHUNK):
        cols = slice(c * FF_CHUNK, (c + 1) * FF_CHUNK)
        gate = jnp.dot(h, wg_ref[:, cols], preferred_element_type=F32)
        up = jnp.dot(h, wu_ref[:, cols], preferred_element_type=F32)
        act = (gate * _sigmoid(gate) * up).astype(BF16)
        acc = acc + jnp.dot(act, wd_ref[cols, :], preferred_element_type=F32)
    return acc


def _ffn_kernel(*refs, n_side):
    x_ref, gn_ref, wg_ref, wu_ref, wd_ref = refs[:5]
    side_refs, o_ref, side_out_refs = refs[5:5 + n_side], refs[5 + n_side], refs[6 + n_side:]
    for side_ref, side_out_ref in zip(side_refs, side_out_refs):
        side_out_ref[...] = side_ref[...].astype(BF16)
    x = x_ref[...]
    h = _rms_norm(x, gn_ref[...]).astype(BF16)
    o_ref[...] = x + _swiglu_rows(h, wg_ref, wu_ref, wd_ref)


def _ffn_dense(x2, gn, wg, wu, wd, sides):
    t, d = x2.shape
    tm = TM_FFN
    side_in, side_out, side_shapes, side_arrays = _side_cast_specs(sides, t // tm)
    out, *sides_bf16 = pl.pallas_call(
        functools.partial(_ffn_kernel, n_side=len(sides)),
        grid=(t // tm,),
        in_specs=[
            pl.BlockSpec((tm, d), lambda i: (i, 0)),
            _const_spec((1, d)),
            _const_spec(wg.shape),
            _const_spec(wu.shape),
            _const_spec(wd.shape),
            *side_in,
        ],
        out_specs=[pl.BlockSpec((tm, d), lambda i: (i, 0)), *side_out],
        out_shape=[jax.ShapeDtypeStruct(x2.shape, F32), *side_shapes],
        compiler_params=pltpu.CompilerParams(
            dimension_semantics=("arbitrary",), vmem_limit_bytes=VMEM_LIMIT_BYTES),
        name="ffn_dense",
    )(x2, gn.reshape(1, -1), wg, wu, wd, *side_arrays)
    return out, sides_bf16


def _split_bf16(a):
    hi = a.astype(BF16)
    return hi, (a - hi.astype(F32)).astype(BF16)


def _router_kernel(*refs, n_side):
    x_ref, gn_ref, wr_hi_ref, wr_lo_ref = refs[:4]
    side_refs, (meta_ref, gates_ref, counts_ref) = refs[4:4 + n_side], refs[4 + n_side:7 + n_side]
    side_out_refs, run = refs[7 + n_side:7 + 2 * n_side], refs[7 + 2 * n_side]
    i = pl.program_id(0)
    tm = x_ref.shape[0]
    for side_ref, side_out_ref in zip(side_refs, side_out_refs):
        side_out_ref[...] = side_ref[...].astype(BF16)

    @pl.when(i == 0)
    def _():
        run[...] = jnp.zeros((1, LANES), F32)

    h_hi, h_lo = _split_bf16(_rms_norm(x_ref[...], gn_ref[...]))
    logits = (jnp.dot(h_hi, wr_hi_ref[...], preferred_element_type=F32)
              + jnp.dot(h_lo, wr_hi_ref[...], preferred_element_type=F32)
              + jnp.dot(h_hi, wr_lo_ref[...], preferred_element_type=F32))
    lane = lax.broadcasted_iota(I32, (tm, LANES), 1)
    neg_inf = jnp.float32(-jnp.inf)
    logits = jnp.where(lane < N_EXPERTS, logits, neg_inf)
    m1 = jnp.max(logits, axis=-1, keepdims=True)
    i1 = jnp.min(jnp.where(logits == m1, lane, LANES), axis=-1, keepdims=True)
    rest = jnp.where(lane == i1, neg_inf, logits)
    m2 = jnp.max(rest, axis=-1, keepdims=True)
    i2 = jnp.min(jnp.where(rest == m2, lane, LANES), axis=-1, keepdims=True)
    e = jnp.exp(m2 - m1)
    g1 = 1.0 / (1.0 + e)
    g2 = e * g1

    sel1 = lane == i1
    sel2 = lane == i2
    hot = jnp.where(sel1 | sel2, 1.0, 0.0)
    rows = lax.broadcasted_iota(I32, (tm, tm), 0)
    cols = lax.broadcasted_iota(I32, (tm, tm), 1)
    strict_lower = jnp.where(cols < rows, 1.0, 0.0).astype(BF16)
    before = jnp.dot(strict_lower, hot.astype(BF16), preferred_element_type=F32) + run[...]
    rank1 = jnp.sum(jnp.where(sel1, before, 0.0), axis=-1, keepdims=True)
    rank2 = jnp.sum(jnp.where(sel2, before, 0.0), axis=-1, keepdims=True)
    run[...] = run[...] + jnp.sum(hot, axis=0, keepdims=True)

    meta = jnp.where(lane == 0, i1, jnp.where(lane == 1, i2, jnp.where(
        lane == 2, rank1.astype(I32), jnp.where(lane == 3, rank2.astype(I32), 0))))
    meta_ref[...] = jnp.transpose(meta)[0:SUBLANES, :]
    gates_ref[...] = jnp.where(lane == 0, g1, jnp.where(lane == 1, g2, 0.0))
    counts_ref[...] = jnp.broadcast_to(run[...], (SUBLANES, LANES))


def _router(x2, gn, w_router, sides):
    t, d = x2.shape
    tm = TM_ROUTE
    wr_hi, wr_lo = _split_bf16(jnp.zeros((d, LANES), F32).at[:, :N_EXPERTS].set(w_router))
    side_in, side_out, side_shapes, side_arrays = _side_cast_specs(sides, t // tm)
    meta, gates, counts, *sides_bf16 = pl.pallas_call(
        functools.partial(_router_kernel, n_side=len(sides)),
        grid=(t // tm,),
        in_specs=[
            pl.BlockSpec((tm, d), lambda i: (i, 0)),
            _const_spec((1, d)),
            _const_spec((d, LANES)),
            _const_spec((d, LANES)),
            *side_in,
        ],
        out_specs=[
            pl.BlockSpec((SUBLANES, tm), lambda i: (0, i)),
            pl.BlockSpec((tm, LANES), lambda i: (i, 0)),
            pl.BlockSpec((SUBLANES, LANES), lambda i: (0, 0)),
            *side_out,
        ],
        out_shape=[
            jax.ShapeDtypeStruct((SUBLANES, t), I32),
            jax.ShapeDtypeStruct((t, LANES), F32),
            jax.ShapeDtypeStruct((SUBLANES, LANES), F32),
            *side_shapes,
        ],
        scratch_shapes=[pltpu.VMEM((1, LANES), F32)],
        compiler_params=pltpu.CompilerParams(
            dimension_semantics=("arbitrary",), vmem_limit_bytes=VMEM_LIMIT_BYTES),
        name="router",
    )(x2, gn.reshape(1, -1), wr_hi, wr_lo, *side_arrays)
    return meta, gates, counts, sides_bf16


def _row_copy(src_ref, src_row, dst_ref, dst_row, sem):
    return pltpu.make_async_copy(src_ref.at[pl.ds(src_row, 1), :], dst_ref.at[pl.ds(dst_row, 1), :], sem)


def _dispatch_kernel(pad_ref, *refs, n_side):
    pos_refs, x_ref, refs = refs[:TOP_K], refs[TOP_K], refs[TOP_K + 1:]
    side_refs, xs_hbm, side_out_refs = refs[:n_side], refs[n_side], refs[n_side + 1:2 * n_side + 1]
    ztile, sem, zsem = refs[2 * n_side + 1:]
    i = pl.program_id(0)
    tile_rows = ztile.shape[0]

    @pl.when(i == 0)
    def _():
        ztile[...] = jnp.zeros(ztile.shape, F32)
        for e in range(N_EXPERTS):
            start = pad_ref[0, e]
            n = pad_ref[1, e]

            def fill(q, c):
                _row_copy(ztile, 0, xs_hbm, start + q, zsem).start()
                return c

            def drain(q, c):
                _row_copy(ztile, 0, xs_hbm, start, zsem).wait()
                return c

            lax.fori_loop(0, n, fill, 0)
            lax.fori_loop(0, n, drain, 0)

        def tail_copy(q):
            r0 = pl.multiple_of(q * tile_rows, tile_rows)
            return pltpu.make_async_copy(ztile, xs_hbm.at[pl.ds(r0, tile_rows), :], zsem)

        def fill_tail(q, c):
            tail_copy(q).start()
            return c

        def drain_tail(q, c):
            tail_copy(q).wait()
            return c

        n_valid = pad_ref[2, 0]
        n_tiles = xs_hbm.shape[0] // tile_rows
        lax.fori_loop(n_valid, n_tiles, fill_tail, 0)
        lax.fori_loop(n_valid, n_tiles, drain_tail, 0)

    def issue(q, c):
        for s in range(SUBLANES):
            for k in range(TOP_K):
                dst = pos_refs[k][0, 0, q * SUBLANES + s]
                pltpu.make_async_copy(x_ref.at[q, pl.ds(s, 1), :], xs_hbm.at[pl.ds(dst, 1), :],
                                      sem.at[k]).start(priority=k)
        return c

    n_groups = x_ref.shape[0]
    lax.fori_loop(0, n_groups, issue, 0)
    for side_ref, side_out_ref in zip(side_refs, side_out_refs):
        side_out_ref[...] = side_ref[...].astype(BF16)
    for k in range(TOP_K):
        for _ in range(n_groups * SUBLANES // tile_rows):
            pltpu.make_async_copy(ztile, xs_hbm.at[pl.ds(0, tile_rows), :], sem.at[k]).wait()


def _dispatch(x2, pos, pad_info, n_rows, sides):
    t, d = x2.shape
    tm = TM_DISP
    n_steps = t // tm
    pos3 = pos.reshape(TOP_K * n_steps, 1, tm)
    side_in, side_out, side_shapes, side_arrays = _side_cast_specs(sides, n_steps)
    grid_spec = pltpu.PrefetchScalarGridSpec(
        num_scalar_prefetch=1,
        grid=(n_steps,),
        in_specs=[
            *[pl.BlockSpec((1, 1, tm), lambda i, pad, k=k: (k * n_steps + i, 0, 0), memory_space=pltpu.SMEM)
              for k in range(TOP_K)],
            pl.BlockSpec((tm // SUBLANES, SUBLANES, d), lambda i, pad: (i, 0, 0)),
            *side_in,
        ],
        out_specs=[pl.BlockSpec(memory_space=pl.ANY), *side_out],
        scratch_shapes=[
            pltpu.VMEM((TM_EXP, d), F32),
            pltpu.SemaphoreType.DMA((TOP_K,)),
            pltpu.SemaphoreType.DMA(()),
        ],
    )
    xs, *sides_bf16 = pl.pallas_call(
        functools.partial(_dispatch_kernel, n_side=len(sides)),
        grid_spec=grid_spec,
        out_shape=[jax.ShapeDtypeStruct((n_rows, d), F32), *side_shapes],
        compiler_params=pltpu.CompilerParams(
            dimension_semantics=("arbitrary",), vmem_limit_bytes=VMEM_LIMIT_BYTES),
        name="dispatch",
    )(pad_info, *[pos3] * TOP_K, x2.reshape(t // SUBLANES, SUBLANES, d), *side_arrays)
    return xs, sides_bf16


def _expert_kernel(tinfo_ref, nvalid_ref, xs_ref, gn_ref, wg_ref, wu_ref, wd_ref, ys_ref):
    i = pl.program_id(0)
    tm = xs_ref.shape[0]
    half = tm // 2
    live = i < nvalid_ref[0]
    rows = tinfo_ref[1, i]

    @pl.when(live & (rows > half))
    def _():
        h = _rms_norm(xs_ref[...], gn_ref[...]).astype(BF16)
        ys_ref[...] = _swiglu_rows(h, wg_ref, wu_ref, wd_ref)

    @pl.when(live & (rows <= half))
    def _():
        h = _rms_norm(xs_ref[0:half, :], gn_ref[...]).astype(BF16)
        ys_ref[0:half, :] = _swiglu_rows(h, wg_ref, wu_ref, wd_ref)
        ys_ref[half:, :] = jnp.zeros((tm - half, ys_ref.shape[1]), F32)

    @pl.when(jnp.logical_not(live))
    def _():
        ys_ref[...] = jnp.zeros(ys_ref.shape, F32)


def _experts(xs, gn, wg, wu, wd, tile_info, n_valid):
    n_rows, d = xs.shape
    tm = TM_EXP
    d_ff = wg.shape[-1]
    last = lambda i, nv: jnp.maximum(jnp.minimum(i, nv[0] - 1), 0)
    tile = lambda i, ti, nv: (last(i, nv), 0)
    expert = lambda i, ti, nv: (ti[0, last(i, nv)], 0, 0)
    grid_spec = pltpu.PrefetchScalarGridSpec(
        num_scalar_prefetch=2,
        grid=(n_rows // tm,),
        in_specs=[
            pl.BlockSpec((tm, d), tile),
            pl.BlockSpec((1, d), lambda i, ti, nv: (0, 0)),
            pl.BlockSpec((None, d, d_ff), expert),
            pl.BlockSpec((None, d, d_ff), expert),
            pl.BlockSpec((None, d_ff, d), expert),
        ],
        out_specs=pl.BlockSpec((tm, d), lambda i, ti, nv: (i, 0)),
    )
    return pl.pallas_call(
        _expert_kernel,
        grid_spec=grid_spec,
        out_shape=jax.ShapeDtypeStruct((n_rows, d), F32),
        compiler_params=pltpu.CompilerParams(
            dimension_semantics=("arbitrary",), vmem_limit_bytes=VMEM_LIMIT_BYTES),
        name="experts",
    )(tile_info, n_valid, xs, gn.reshape(1, -1), wg, wu, wd)


def _combine_kernel(*refs):
    pos0_refs, pos_next_refs = refs[:TOP_K], refs[TOP_K:2 * TOP_K]
    x_ref, gates_ref, gf_ref, ys_rows, ys_hbm, o_ref, buf, sem = refs[2 * TOP_K:]
    i = pl.program_id(0)
    n_groups = buf.shape[2]
    tm = n_groups * SUBLANES
    d = buf.shape[4]

    def gather(pos_refs, slot):
        def issue(q, c):
            for s in range(SUBLANES):
                for k in range(TOP_K):
                    src = pos_refs[k][0, 0, q * SUBLANES + s]
                    pltpu.make_async_copy(ys_rows.at[pl.ds(src, 1), :],
                                          buf.at[slot, k, q, pl.ds(s, 1), :], sem.at[slot, k]).start(priority=k)
            return c

        lax.fori_loop(0, n_groups, issue, 0)

    @pl.when(i == 0)
    def _():
        gather(pos0_refs, 0)

    @pl.when(i + 1 < pl.num_programs(0))
    def _():
        gather(pos_next_refs, (i + 1) % 2)

    slot = i % 2
    for k in range(TOP_K):
        pltpu.make_async_copy(ys_hbm.at[pl.ds(0, n_groups)], buf.at[slot, k], sem.at[slot, k]).wait()
    gates = gates_ref[...]
    y = (x_ref[...] + gates[:, 0:1] * buf[slot, 0].reshape(tm, d) + gates[:, 1:2] * buf[slot, 1].reshape(tm, d))
    o_ref[...] = _rms_norm(y, gf_ref[...])


def _combine(x2, gates, pos, ys, g_final):
    t, d = x2.shape
    tm = TM_DISP
    n_steps = t // tm
    pos3 = pos.reshape(TOP_K * n_steps, 1, tm)
    pos_spec = lambda tile_of_step: [
        pl.BlockSpec((1, 1, tm), lambda i, k=k: (k * n_steps + tile_of_step(i), 0, 0), memory_space=pltpu.SMEM)
        for k in range(TOP_K)]
    return pl.pallas_call(
        _combine_kernel,
        grid=(n_steps,),
        in_specs=[
            *pos_spec(lambda i: 0),
            *pos_spec(lambda i: jnp.minimum(i + 1, n_steps - 1)),
            pl.BlockSpec((tm, d), lambda i: (i, 0)),
            pl.BlockSpec((tm, LANES), lambda i: (i, 0)),
            _const_spec((1, d)),
            pl.BlockSpec(memory_space=pl.ANY),
            pl.BlockSpec(memory_space=pl.ANY),
        ],
        out_specs=pl.BlockSpec((tm, d), lambda i: (i, 0)),
        out_shape=jax.ShapeDtypeStruct(x2.shape, F32),
        scratch_shapes=[pltpu.VMEM((2, TOP_K, tm // SUBLANES, SUBLANES, d), F32),
                        pltpu.SemaphoreType.DMA((2, TOP_K))],
        compiler_params=pltpu.CompilerParams(
            dimension_semantics=("arbitrary",), vmem_limit_bytes=VMEM_LIMIT_BYTES),
        name="combine",
    )(*[pos3] * (2 * TOP_K), x2, gates, g_final.reshape(1, -1), ys,
      ys.reshape(ys.shape[0] // SUBLANES, SUBLANES, d))


def _moe_layer(x2, gn, w_router, wg, wu_f32, wd, g_final):
    t, _ = x2.shape
    tm = TM_EXP
    n_tiles = (TOP_K * t) // tm + N_EXPERTS
    meta, gates, counts, _ = _router(x2, gn, w_router, [])

    counts = counts[0, :N_EXPERTS].astype(I32)
    padded = ((counts + tm - 1) // tm) * tm
    ends = jnp.cumsum(padded)
    starts = ends - padded
    idx = meta[0:TOP_K, :]
    rank = meta[TOP_K:2 * TOP_K, :]
    group_start = sum(jnp.where(idx == e, starts[e], 0) for e in range(N_EXPERTS))
    pos = group_start + rank
    tile_start = jnp.arange(n_tiles, dtype=I32) * tm
    tile_expert = jnp.minimum(
        jnp.sum((tile_start[:, None] >= ends[None, :]).astype(I32), axis=1), N_EXPERTS - 1).astype(I32)
    token_end = sum(jnp.where(tile_expert == e, starts[e] + counts[e], 0) for e in range(N_EXPERTS))
    tile_info = jnp.stack([tile_expert, jnp.clip(token_end - tile_start, 0, tm)]).astype(I32)
    n_valid = (ends[-1:] // tm).astype(I32)
    pad_info = jnp.stack([starts + counts, padded - counts, jnp.broadcast_to(n_valid, (N_EXPERTS,))]).astype(I32)

    xs, (wu,) = _dispatch(x2, pos, pad_info, n_tiles * tm, [wu_f32.reshape(-1, wu_f32.shape[-1])])
    ys = _experts(xs, gn, wg, wu.reshape(wu_f32.shape), wd, tile_info, n_valid)
    return _combine(x2, gates, pos, ys, g_final)


def kernel(x, norm_mix, norm_ffn, norm_final, w_in, conv_w, lru_conv_w, lru_conv_b, lru_wa, lru_ba, lru_wx,
           lru_bx, lru_lambda, w_out, ffn_w_gate, ffn_w_up, ffn_w_down, w_router, moe_w_gate, moe_w_up,
           moe_w_down):
    bsz, seq, d = x.shape
    depth = w_in.shape[0]
    assert depth == 2, "layer 0 uses the dense FFN, layer 1 the routed FFN followed by the final norm"
    x2 = x.reshape(bsz * seq, d)

    def mixer(x2, l, w_in_l, w_out_l, sides):
        return _mixer(x2, seq, l == 0, l == depth - 1, norm_mix[l], w_in_l, conv_w[l], lru_conv_w[l],
                      lru_conv_b[l], lru_wa[l], lru_ba[l], lru_wx[l], lru_bx[l], lru_lambda[l], w_out_l, sides)

    flat = lambda w: w.reshape(-1, w.shape[-1])
    x2, (ffn_gate, ffn_up, ffn_down) = mixer(
        x2, 0, w_in[0].astype(BF16), w_out[0].astype(BF16), [ffn_w_gate[0], ffn_w_up[0], ffn_w_down[0]])
    x2, (moe_gate, moe_down, w_in_1, w_out_1) = _ffn_dense(
        x2, norm_ffn[0], ffn_gate, ffn_up, ffn_down,
        [flat(moe_w_gate[0]), flat(moe_w_down[0]), (w_in, 1), (w_out, 1)])
    x2, _ = mixer(x2, 1, w_in_1, w_out_1, [])
    x2 = _moe_layer(x2, norm_ffn[1], w_router[0], moe_gate.reshape(moe_w_gate[0].shape), moe_w_up[0],
                    moe_down.reshape(moe_w_down[0].shape), norm_final)
    return x2.reshape(bsz, seq, d)
```
